```python
import math
import jax
import jax.numpy as jnp
from jax import lax
import numpy as np

D_MODEL = 2048
BATCH = 4
SEQ = 2048
DEPTH = 4

GRID_W = 64
CTX_LEN = 256
N_MIXERS = 2
N_ATTN_LAYERS = (DEPTH + N_MIXERS - 1) // N_MIXERS
N_SSM_LAYERS = DEPTH // N_MIXERS
NORM_EPS = 1e-6
ADA_CHUNKS = 6

DA_HEADS = 8
DA_HEAD_DIM = 128
DA_V_DIM = 2 * DA_HEAD_DIM
DA_QK_WIDTH = DA_HEADS * 2 * DA_HEAD_DIM
DA_V_WIDTH = DA_HEADS * DA_V_DIM
DA_SCALE = DA_HEAD_DIM ** -0.5
SUBLN_EPS = 1e-5
ROPE_BASE = 10000.0
Q_BLOCK = 128

SSM_GROUP = 16
SSM_GROUPS = D_MODEL // SSM_GROUP
SSM_STATE = 64
DT_MIN = 0.001
DT_MAX = 0.1

MOE_GROUPS = 4
MOE_EXPERTS_PER_GROUP = 8
MOE_EXPERTS = MOE_GROUPS * MOE_EXPERTS_PER_GROUP
MOE_TOPK = 2
MOE_HIDDEN = 512
MOE_BLOCK = 128

kernel_name = 'hybrid_diffattn_s5_hmoe_prefix_dit'


def rms_norm(x, w, eps=NORM_EPS):
    xf = x.astype(jnp.float32)
    y = xf * lax.rsqrt(jnp.mean(xf * xf, axis=-1, keepdims=True) + eps)
    return (y * w.astype(jnp.float32)).astype(x.dtype)


def modulate(h, shift, scale):
    return h * (1.0 + scale) + shift


def rope_axis(x, pos):
    half = x.shape[-1] // 2
    inv_freq = ROPE_BASE ** (-jnp.arange(half, dtype=jnp.float32) / half)
    ang = pos[:, None] * inv_freq[None, :]
    cos, sin = jnp.cos(ang), jnp.sin(ang)
    xf = x.astype(jnp.float32)
    x1, x2 = xf[..., :half], xf[..., half:]
    return jnp.concatenate([x1 * cos - x2 * sin, x2 * cos + x1 * sin], axis=-1).astype(x.dtype)


def rope2d(x, row, col):
    r = x.shape[-1] // 2
    return jnp.concatenate([rope_axis(x[..., :r], row), rope_axis(x[..., r:], col)], axis=-1)


def diff_attention(h_lat, h_ctx, w_qkv, q_norm, k_norm, lam_q1, lam_k1, lam_q2, lam_k2,
                   subln_w, w_o, lam_init, row, col, need_ctx):
    bsz, n_lat, _ = h_lat.shape

    def project(h, rotary):
        n = h.shape[1]
        q, k, v = jnp.split(h @ w_qkv, [DA_QK_WIDTH, 2 * DA_QK_WIDTH], axis=-1)
        q = rms_norm(q.reshape(bsz, n, DA_HEADS, 2, DA_HEAD_DIM), q_norm).transpose(0, 2, 3, 1, 4)
        k = rms_norm(k.reshape(bsz, n, DA_HEADS, 2, DA_HEAD_DIM), k_norm).transpose(0, 2, 3, 1, 4)
        v = v.reshape(bsz, n, DA_HEADS, DA_V_DIM).transpose(0, 2, 1, 3)
        if rotary:
            q = rope2d(q, row, col)
            k = rope2d(k, row, col)
        return q, k, v

    lam = (jnp.exp(jnp.sum(lam_q1.astype(jnp.float32) * lam_k1.astype(jnp.float32)))
           - jnp.exp(jnp.sum(lam_q2.astype(jnp.float32) * lam_k2.astype(jnp.float32)))
           + lam_init)

    def attend(q, k, v):
        s = jnp.einsum('bhmqd,bhmkd->bhmqk', q, k).astype(jnp.float32) * DA_SCALE
        p = jax.nn.softmax(s, axis=-1)
        a = p[:, :, 0] - lam * p[:, :, 1]
        return jnp.einsum('bhqk,bhkv->bhqv', a, v.astype(jnp.float32))

    def finish(o):
        o = rms_norm(o, subln_w, SUBLN_EPS) * (1.0 - lam_init)
        n = o.shape[2]
        return o.transpose(0, 2, 1, 3).reshape(bsz, n, DA_V_WIDTH).astype(h_lat.dtype) @ w_o

    q_c, k_c, v_c = project(h_ctx, False)
    q_l, k_l, v_l = project(h_lat, True)
    k_all = jnp.concatenate([k_l, k_c], axis=3)
    v_all = jnp.concatenate([v_l, v_c], axis=2)
    n_blk = n_lat // Q_BLOCK
    q_blocks = q_l.reshape(bsz, DA_HEADS, 2, n_blk, Q_BLOCK, DA_HEAD_DIM).transpose(3, 0, 1, 2, 4, 5)
    o_blocks = lax.map(lambda qb: attend(qb, k_all, v_all), q_blocks)
    o_lat = o_blocks.transpose(1, 2, 0, 3, 4).reshape(bsz, DA_HEADS, n_lat, DA_V_DIM)
    y_lat = finish(o_lat)
    y_ctx = finish(attend(q_c, k_c, v_c)) if need_ctx else None
    return y_lat, y_ctx


def s5_discretize(a_re, a_im, log_dt, b_re, b_im):
    a_re = jnp.minimum(a_re.astype(jnp.float32), -1e-4)
    a_im = a_im.astype(jnp.float32)
    dt = jnp.exp(log_dt.astype(jnp.float32))[:, None]
    mag = jnp.exp(a_re * dt)
    abar_re = mag * jnp.cos(a_im * dt)
    abar_im = mag * jnp.sin(a_im * dt)
    den = a_re * a_re + a_im * a_im
    f_re = (((abar_re - 1.0) * a_re + abar_im * a_im) / den)[..., None]
    f_im = ((abar_im * a_re - (abar_re - 1.0) * a_im) / den)[..., None]
    b_re = b_re.astype(jnp.float32)
    b_im = b_im.astype(jnp.float32)
    return abar_re, abar_im, f_re * b_re - f_im * b_im, f_re * b_im + f_im * b_re


def complex_combine(e1, e2):
    a1r, a1i, b1r, b1i = e1
    a2r, a2i, b2r, b2i = e2
    return (a1r * a2r - a1i * a2i, a1r * a2i + a1i * a2r,
            a2r * b1r - a2i * b1i + b2r, a2r * b1i + a2i * b1r + b2i)


def diag_scan(abar_re, abar_im, bu_re, bu_im, h0_re=None, h0_im=None):
    if h0_re is not None:
        bu_re = bu_re.at[0].add(abar_re * h0_re - abar_im * h0_im)
        bu_im = bu_im.at[0].add(abar_re * h0_im + abar_im * h0_re)
    shape = (bu_re.shape[0], 1) + abar_re.shape
    a_re = jnp.broadcast_to(abar_re, shape)
    a_im = jnp.broadcast_to(abar_im, shape)
    _, _, s_re, s_im = lax.associative_scan(complex_combine, (a_re, a_im, bu_re, bu_im), axis=0)
    return s_re, s_im


def s5_mixer(h_lat, h_ctx, a_re, a_im, log_dt, b_re, b_im, c_re, c_im, d_skip, w_glu, b_glu, need_ctx):
    bsz, n_lat, _ = h_lat.shape
    n_ctx = h_ctx.shape[1]
    u_lat = h_lat.reshape(bsz, n_lat, SSM_GROUPS, SSM_GROUP)
    u_ctx = h_ctx.reshape(bsz, n_ctx, SSM_GROUPS, SSM_GROUP)
    y_lat = d_skip * h_lat
    y_ctx = d_skip * h_ctx if need_ctx else None
    for direction in range(2):
        abr, abi, bbr, bbi = s5_discretize(a_re[direction], a_im[direction], log_dt[direction],
                                           b_re[direction], b_im[direction])
        cr, ci = c_re[direction], c_im[direction]

        def drive(u):
            return (jnp.einsum('bngc,gpc->nbgp', u, bbr), jnp.einsum('bngc,gpc->nbgp', u, bbi))

        def readout(s_re, s_im):
            y = jnp.einsum('nbgp,gcp->bngc', s_re, cr) - jnp.einsum('nbgp,gcp->bngc', s_im, ci)
            return y.reshape(bsz, -1, D_MODEL)

        uc = u_ctx if direction == 0 else u_ctx[:, ::-1]
        ul = u_lat if direction == 0 else u_lat[:, ::-1]
        sc_re, sc_im = diag_scan(abr, abi, *drive(uc))
        sl_re, sl_im = diag_scan(abr, abi, *drive(ul), sc_re[-1], sc_im[-1])
        yl = readout(sl_re, sl_im)
        y_lat = y_lat + (yl if direction == 0 else yl[:, ::-1])
        if need_ctx:
            yc = readout(sc_re, sc_im)
            y_ctx = y_ctx + (yc if direction == 0 else yc[:, ::-1])

    def glu(y):
        za, zb = jnp.split(jax.nn.gelu(y) @ w_glu + b_glu, 2, axis=-1)
        return (za * jax.nn.sigmoid(zb)).astype(h_lat.dtype)

    return glu(y_lat), (glu(y_ctx) if need_ctx else None)


def hier_moe(h, w_rg, b_rg, w_re, b_re, w1, w3, w2):
    n_tok, d = h.shape
    g_prob = jax.nn.softmax((h @ w_rg + b_rg).astype(jnp.float32), axis=-1)
    g_p, g_idx = lax.top_k(g_prob, 1)
    e_logits = (jnp.einsum('td,dge->tge', h, w_re) + b_re).astype(jnp.float32)
    e_logits = jnp.take_along_axis(e_logits, g_idx[:, :, None], axis=1)[:, 0]
    e_top, e_idx = lax.top_k(e_logits, MOE_TOPK)
    e_w = jax.nn.softmax(e_top, axis=-1) * g_p
    flat_e = (g_idx * MOE_EXPERTS_PER_GROUP + e_idx).reshape(-1)
    flat_w = e_w.reshape(-1)
    flat_t = jnp.repeat(jnp.arange(n_tok, dtype=jnp.int32), MOE_TOPK)
    n_assign = n_tok * MOE_TOPK
    counts = jnp.zeros((MOE_EXPERTS,), jnp.int32).at[flat_e].add(1)
    padded = (counts + MOE_BLOCK - 1) // MOE_BLOCK * MOE_BLOCK
    pad_end = jnp.cumsum(padded)
    pad_start = pad_end - padded
    start = jnp.cumsum(counts) - counts
    order = jnp.argsort(flat_e)
    sorted_e = flat_e[order]
    dest = pad_start[sorted_e] + jnp.arange(n_assign, dtype=jnp.int32) - start[sorted_e]
    n_blocks = -(-n_assign // MOE_BLOCK) + MOE_EXPERTS
    n_rows = n_blocks * MOE_BLOCK
    row_token = jnp.full((n_rows,), n_tok, jnp.int32).at[dest].set(flat_t[order])
    row_w = jnp.zeros((n_rows,), jnp.float32).at[dest].set(flat_w[order])
    block_exp = jnp.minimum(
        jnp.searchsorted(pad_end, jnp.arange(n_blocks, dtype=jnp.int32) * MOE_BLOCK, side='right'),
        MOE_EXPERTS - 1)
    h_pad = jnp.concatenate([h, jnp.zeros((1, d), h.dtype)], axis=0)
    xs = h_pad[row_token].reshape(n_blocks, MOE_BLOCK, d)
    w1f = w1.reshape(MOE_EXPERTS, d, MOE_HIDDEN)
    w3f = w3.reshape(MOE_EXPERTS, d, MOE_HIDDEN)
    w2f = w2.reshape(MOE_EXPERTS, MOE_HIDDEN, d)

    def expert_block(args):
        xb, e = args
        return (jax.nn.silu(xb @ w1f[e]) * (xb @ w3f[e])) @ w2f[e]

    ys = lax.map(expert_block, (xs, block_exp)).reshape(n_rows, d)
    out = jnp.zeros((n_tok + 1, d), jnp.float32).at[row_token].add(ys.astype(jnp.float32) * row_w[:, None])
    return out[:n_tok].astype(h.dtype)


def setup_inputs(seed: int = 0) -> dict:
    key = jax.random.key(seed)
    keys = jax.random.split(key, 34)

    def nrm(i, shape, scale):
        return jax.random.normal(keys[i], shape, jnp.float32) * scale

    D = D_MODEL
    NA, NS = N_ATTN_LAYERS, N_SSM_LAYERS
    G, P, CH = SSM_GROUPS, SSM_STATE, SSM_GROUP
    NG, E, F = MOE_GROUPS, MOE_EXPERTS_PER_GROUP, MOE_HIDDEN
    a_im_init = jnp.pi * jnp.arange(P, dtype=jnp.float32)
    return {
        'x': nrm(0, (BATCH, SEQ, D), 1.0),
        'c': nrm(1, (BATCH, D), 1.0),
        'ctx': nrm(2, (BATCH, CTX_LEN, D), 1.0),
        'c_ctx': nrm(3, (D,), 1.0),
        'ada_w': nrm(4, (DEPTH, D, ADA_CHUNKS * D), 0.5 * D ** -0.5),
        'ada_b': nrm(5, (DEPTH, ADA_CHUNKS * D), 0.02),
        'norm1_w': 1.0 + nrm(6, (DEPTH, D), 0.02),
        'norm2_w': 1.0 + nrm(7, (DEPTH, D), 0.02),
        'attn_w_qkv': nrm(8, (NA, D, 2 * DA_QK_WIDTH + DA_V_WIDTH), D ** -0.5),
        'attn_q_norm': 1.0 + nrm(9, (NA, DA_HEAD_DIM), 0.02),
        'attn_k_norm': 1.0 + nrm(10, (NA, DA_HEAD_DIM), 0.02),
        'attn_lam_q1': nrm(11, (NA, DA_HEAD_DIM), 0.1),
        'attn_lam_k1': nrm(12, (NA, DA_HEAD_DIM), 0.1),
        'attn_lam_q2': nrm(13, (NA, DA_HEAD_DIM), 0.1),
        'attn_lam_k2': nrm(14, (NA, DA_HEAD_DIM), 0.1),
        'attn_subln': 1.0 + nrm(15, (NA, DA_V_DIM), 0.02),
        'attn_w_o': nrm(16, (NA, DA_V_WIDTH, D), DA_V_WIDTH ** -0.5),
        'ssm_a_re': -0.5 + nrm(17, (NS, 2, G, P), 0.01),
        'ssm_a_im': a_im_init + nrm(18, (NS, 2, G, P), 0.01),
        'ssm_log_dt': jax.random.uniform(keys[19], (NS, 2, G), jnp.float32,
                                         minval=math.log(DT_MIN), maxval=math.log(DT_MAX)),
        'ssm_b_re': nrm(20, (NS, 2, G, P, CH), (2 * CH) ** -0.5),
        'ssm_b_im': nrm(21, (NS, 2, G, P, CH), (2 * CH) ** -0.5),
        'ssm_c_re': nrm(22, (NS, 2, G, CH, P), P ** -0.5),
        'ssm_c_im': nrm(23, (NS, 2, G, CH, P), P ** -0.5),
        'ssm_d': nrm(24, (NS, D), 1.0),
        'ssm_w_glu': nrm(25, (NS, D, 2 * D), D ** -0.5),
        'ssm_b_glu': nrm(26, (NS, 2 * D), 0.02),
        'moe_w_rg': nrm(27, (DEPTH, D, NG), D ** -0.5),
        'moe_b_rg': nrm(28, (DEPTH, NG), 0.01),
        'moe_w_re': nrm(29, (DEPTH, D, NG, E), D ** -0.5),
        'moe_b_re': nrm(30, (DEPTH, NG, E), 0.01),
        'moe_w1': nrm(31, (DEPTH, NG, E, D, F), D ** -0.5),
        'moe_w3': nrm(32, (DEPTH, NG, E, D, F), D ** -0.5),
        'moe_w2': nrm(33, (DEPTH, NG, E, F, D), F ** -0.5),
    }


def reference(x, c, ctx, c_ctx, ada_w, ada_b, norm1_w, norm2_w,
              attn_w_qkv, attn_q_norm, attn_k_norm, attn_lam_q1, attn_lam_k1, attn_lam_q2, attn_lam_k2,
              attn_subln, attn_w_o,
              ssm_a_re, ssm_a_im, ssm_log_dt, ssm_b_re, ssm_b_im, ssm_c_re, ssm_c_im, ssm_d,
              ssm_w_glu, ssm_b_glu,
              moe_w_rg, moe_b_rg, moe_w_re, moe_b_re, moe_w1, moe_w3, moe_w2):
    bsz, n_lat, d = x.shape
    n_rows = n_lat // GRID_W
    row = jnp.repeat(jnp.arange(n_rows), GRID_W).astype(jnp.float32)
    col = jnp.tile(jnp.arange(GRID_W), n_rows).astype(jnp.float32)
    silu_c = jax.nn.silu(c)
    silu_cc = jax.nn.silu(c_ctx)
    for i in range(DEPTH):
        need_ctx = i < DEPTH - 1
        j = i // N_MIXERS
        mod = silu_c @ ada_w[i] + ada_b[i]
        sh1, sc1, g1, sh2, sc2, g2 = jnp.split(mod[:, None, :], ADA_CHUNKS, axis=-1)
        mod_c = silu_cc @ ada_w[i] + ada_b[i]
        csh1, csc1, cg1, csh2, csc2, cg2 = jnp.split(mod_c, ADA_CHUNKS, axis=-1)

        h_lat = modulate(rms_norm(x, norm1_w[i]), sh1, sc1)
        h_ctx = modulate(rms_norm(ctx, norm1_w[i]), csh1, csc1)
        if i % N_MIXERS == 0:
            y_lat, y_ctx = diff_attention(
                h_lat, h_ctx, attn_w_qkv[j], attn_q_norm[j], attn_k_norm[j],
                attn_lam_q1[j], attn_lam_k1[j], attn_lam_q2[j], attn_lam_k2[j],
                attn_subln[j], attn_w_o[j], 0.8 - 0.6 * math.exp(-0.3 * i), row, col, need_ctx)
        else:
            y_lat, y_ctx = s5_mixer(
                h_lat, h_ctx, ssm_a_re[j], ssm_a_im[j], ssm_log_dt[j], ssm_b_re[j], ssm_b_im[j],
                ssm_c_re[j], ssm_c_im[j], ssm_d[j], ssm_w_glu[j], ssm_b_glu[j], need_ctx)
        x = x + g1 * y_lat
        if need_ctx:
            ctx = ctx + cg1 * y_ctx

        hl = modulate(rms_norm(x, norm2_w[i]), sh2, sc2)
        if need_ctx:
            hc = modulate(rms_norm(ctx, norm2_w[i]), csh2, csc2)
            tokens = jnp.concatenate([hl, hc], axis=1)
        else:
            tokens = hl
        n_tok = tokens.shape[1]
        y = hier_moe(tokens.reshape(bsz * n_tok, d), moe_w_rg[i], moe_b_rg[i], moe_w_re[i], moe_b_re[i],
                     moe_w1[i], moe_w3[i], moe_w2[i]).reshape(bsz, n_tok, d)
        x = x + g2 * y[:, :n_lat]
        if need_ctx:
            ctx = ctx + cg2 * y[:, n_lat:]
    return x
```

```python
import functools
import math

import jax
import jax.numpy as jnp
import numpy as np
from jax import lax
from jax.experimental import pallas as pl
from jax.experimental.pallas import tpu as pltpu

F32 = jnp.float32
BF16 = jnp.bfloat16

D = 2048
B = 4
SEQ = 2048
CTX = 256
NZ = CTX + SEQ
T = B * NZ
DEPTH = 4
GRID_W = 64
NORM_EPS = 1e-6
ADA_CHUNKS = 6
CTX_MOD_ROW = B
MOD_ROWS = 8

HEADS = 8
HEAD_DIM = 128
V_DIM = 2 * HEAD_DIM
QK_WIDTH = HEADS * 2 * HEAD_DIM
DA_SCALE = HEAD_DIM ** -0.5
SUBLN_EPS = 1e-5
ROPE_BASE = 10000.0

SSM_CH = 16
SSM_GROUPS = D // SSM_CH
SSM_STATE = 64
LANE = 128
SUBLANE = 8
GROUPS_PER_LANE_BLOCK = LANE // SSM_CH
N_LANE_BLOCKS = D // LANE
STATE_LANES = GROUPS_PER_LANE_BLOCK * SSM_STATE
SCAN_SEQS = 2 * B
SCAN_SUB = 32
SCAN_ROWS = SCAN_SUB * SCAN_SEQS

N_GROUPS = 4
N_EPG = 8
N_EXPERTS = N_GROUPS * N_EPG
MOE_F = 512
MOE_TOPK = 2
ROUTE_LANES = 128
TE = 256
N_EBLOCKS = (T * MOE_TOPK) // TE + N_EXPERTS
N_EROWS = N_EBLOCKS * TE

TM_MM = 1152
TN_MM = 512
NORM_ROWS = 128
TM_EW = 576
TM_RT = 256
VMEM_LIMIT = 56 * 1024 * 1024


def _cparams(sem):
    return pltpu.CompilerParams(dimension_semantics=sem, vmem_limit_bytes=VMEM_LIMIT)


def _mod_specs(chunk, tm, tn=None):
    if tn is None:
        return (pl.BlockSpec((1, 1, D), lambda i, *_: (((i * tm) // NZ) * ADA_CHUNKS + chunk, 0, 0)),
                pl.BlockSpec((1, 1, D), lambda i, *_: (CTX_MOD_ROW * ADA_CHUNKS + chunk, 0, 0)))
    return (pl.BlockSpec((1, 1, tn), lambda i, j: (((i * tm) // NZ) * ADA_CHUNKS + chunk, 0, j)),
            pl.BlockSpec((1, 1, tn), lambda i, j: (CTX_MOD_ROW * ADA_CHUNKS + chunk, 0, j)))


def _is_ctx_rows(tm):
    z0 = (pl.program_id(0) * tm) % NZ
    return (z0 + lax.broadcasted_iota(jnp.int32, (tm, 1), 0)) < CTX


def _pick(is_ctx, b_ref, c_ref):
    return jnp.where(is_ctx, c_ref[0], b_ref[0])


def _norm_mod(x, nw, sh, sc):
    y = x * lax.rsqrt(jnp.mean(x * x, axis=-1, keepdims=True) + NORM_EPS) * nw
    return y * (1.0 + sc) + sh


ADA_TN = 512
ADA_ROWS = B + 1


def _ada_kernel(ct_ref, w_ref, b_ref, o_ref, s_scr):
    @pl.when((pl.program_id(0) == 0) & (pl.program_id(1) == 0))
    def _():
        c = ct_ref[...]
        s = jax.nn.silu(c)
        for r in range(ADA_ROWS):
            s_scr[r] = jnp.broadcast_to(s[:, r:r + 1], (D, LANE))

    nj = ADA_TN // LANE

    def body(kb, accs):
        k0 = pl.multiple_of(kb * SUBLANE, SUBLANE)
        wk = w_ref[0, pl.ds(k0, SUBLANE), :]
        new = []
        for r in range(ADA_ROWS):
            sk = s_scr[r, pl.ds(k0, SUBLANE), :]
            for j in range(nj):
                new.append(accs[r * nj + j] + wk[:, j * LANE:(j + 1) * LANE] * sk)
        return tuple(new)

    zero = jnp.zeros((SUBLANE, LANE), F32)
    accs = lax.fori_loop(0, D // SUBLANE, body, (zero,) * (ADA_ROWS * nj), unroll=4)
    rows = []
    for r in range(ADA_ROWS):
        rows.append(jnp.concatenate(
            [jnp.sum(accs[r * nj + j], axis=0, keepdims=True) for j in range(nj)], axis=1))
    rows.append(jnp.zeros((MOD_ROWS - ADA_ROWS, ADA_TN), F32))
    o_ref[0] = jnp.concatenate(rows, axis=0) + b_ref[0]


def _ada_table(c, c_ctx, ada_w, ada_b):
    cs = jnp.concatenate([c, c_ctx[None, :], jnp.zeros((LANE - ADA_ROWS, D), F32)], axis=0)
    ct = cs.T
    n_out = ADA_CHUNKS * D
    return pl.pallas_call(
        _ada_kernel,
        grid=(DEPTH, n_out // ADA_TN),
        in_specs=[pl.BlockSpec((D, LANE), lambda l, j: (0, 0)),
                  pl.BlockSpec((1, D, ADA_TN), lambda l, j: (l, 0, j)),
                  pl.BlockSpec((1, 1, ADA_TN), lambda l, j: (l, 0, j))],
        out_specs=pl.BlockSpec((1, MOD_ROWS, ADA_TN), lambda l, j: (l, 0, j)),
        out_shape=jax.ShapeDtypeStruct((DEPTH, MOD_ROWS, n_out), F32),
        scratch_shapes=[pltpu.VMEM((ADA_ROWS, D, LANE), F32)],
        compiler_params=_cparams(("arbitrary", "arbitrary")),
    )(ct, ada_w, ada_b.reshape(DEPTH, 1, n_out))


def _rope_tables():
    half = HEAD_DIM // 4
    inv_freq = ROPE_BASE ** (-np.arange(half, dtype=np.float32) / half)
    t = np.arange(SEQ)
    row = (t // GRID_W).astype(np.float32)[:, None] * inv_freq[None, :]
    col = (t % GRID_W).astype(np.float32)[:, None] * inv_freq[None, :]
    cos_l = np.concatenate([np.cos(row), np.cos(row), np.cos(col), np.cos(col)], axis=1)
    sin_l = np.concatenate([-np.sin(row), np.sin(row), -np.sin(col), np.sin(col)], axis=1)
    cos = np.concatenate([np.ones((CTX, HEAD_DIM), np.float32), cos_l.astype(np.float32)], axis=0)
    sin = np.concatenate([np.zeros((CTX, HEAD_DIM), np.float32), sin_l.astype(np.float32)], axis=0)
    return jnp.asarray(cos), jnp.asarray(sin)


def _qkv_kernel(x_ref, nw_ref, shb_ref, shc_ref, scb_ref, scc_ref, w_ref, qn_ref, kn_ref,
                cos_ref, sin_ref, o_ref, h_scr):
    j = pl.program_id(1)

    @pl.when(j == 0)
    def _():
        z0 = (pl.program_id(0) * TM_MM) % NZ
        for r0 in range(0, TM_MM, NORM_ROWS):
            rows = slice(r0, r0 + NORM_ROWS)
            is_ctx = (z0 + r0 + lax.broadcasted_iota(jnp.int32, (NORM_ROWS, 1), 0)) < CTX
            h = _norm_mod(x_ref[rows, :], nw_ref[...], _pick(is_ctx, shb_ref, shc_ref),
                          _pick(is_ctx, scb_ref, scc_ref))
            h_scr[rows, :] = h.astype(BF16)

    acc = jnp.dot(h_scr[...], w_ref[...].astype(BF16), preferred_element_type=F32)
    n_q = QK_WIDTH // TN_MM

    @pl.when(j < 2 * n_q)
    def _():
        is_q = j < n_q
        nw = jnp.where(is_q, qn_ref[...], kn_ref[...])
        post = jnp.where(is_q, DA_SCALE, 1.0)
        cos = cos_ref[...]
        sin = sin_ref[...]
        lane = lax.broadcasted_iota(jnp.int32, (TM_MM, HEAD_DIM), 1)
        first_half = (lane % (HEAD_DIM // 2)) < (HEAD_DIM // 4)
        outs = []
        for c in range(TN_MM // HEAD_DIM):
            a = acc[:, c * HEAD_DIM:(c + 1) * HEAD_DIM]
            a = a * lax.rsqrt(jnp.mean(a * a, axis=-1, keepdims=True) + NORM_EPS) * nw
            up = pltpu.roll(a, HEAD_DIM - HEAD_DIM // 4, 1)
            dn = pltpu.roll(a, HEAD_DIM // 4, 1)
            a = a * cos + jnp.where(first_half, up, dn) * sin
            outs.append((a * post).astype(BF16))
        o_ref[...] = jnp.concatenate(outs, axis=1)

    @pl.when(j >= 2 * n_q)
    def _():
        o_ref[...] = acc.astype(BF16)


def _qkv(xs, mods, norm_w, w_qkv, q_norm, k_norm, cos, sin):
    n_out = w_qkv.shape[1]
    shb, shc = _mod_specs(0, TM_MM)
    scb, scc = _mod_specs(1, TM_MM)
    tiles_per_batch = NZ // TM_MM
    return pl.pallas_call(
        _qkv_kernel,
        grid=(T // TM_MM, n_out // TN_MM),
        in_specs=[pl.BlockSpec((TM_MM, D), lambda i, j: (i, 0)),
                  pl.BlockSpec((1, D), lambda i, j: (0, 0)),
                  shb, shc, scb, scc,
                  pl.BlockSpec((D, TN_MM), lambda i, j: (0, j)),
                  pl.BlockSpec((1, HEAD_DIM), lambda i, j: (0, 0)),
                  pl.BlockSpec((1, HEAD_DIM), lambda i, j: (0, 0)),
                  pl.BlockSpec((TM_MM, HEAD_DIM), lambda i, j: (i % tiles_per_batch, 0)),
                  pl.BlockSpec((TM_MM, HEAD_DIM), lambda i, j: (i % tiles_per_batch, 0))],
        out_specs=pl.BlockSpec((TM_MM, TN_MM), lambda i, j: (i, j)),
        out_shape=jax.ShapeDtypeStruct((T, n_out), BF16),
        scratch_shapes=[pltpu.VMEM((TM_MM, D), BF16)],
        compiler_params=_cparams(("arbitrary", "arbitrary")),
    )(xs, norm_w.reshape(1, D), mods, mods, mods, mods, w_qkv,
      q_norm.reshape(1, HEAD_DIM), k_norm.reshape(1, HEAD_DIM), cos, sin)


TQ = 256
Q_TILES = NZ // TQ


def _attn_kernel(lam_ref, q_ref, k_ref, v_ref, sub_ref, o_ref, *, lam_init):
    lv = lam_ref[...]
    lam = (jnp.exp(jnp.sum(lv[0:1] * lv[1:2], axis=-1, keepdims=True))
           - jnp.exp(jnp.sum(lv[2:3] * lv[3:4], axis=-1, keepdims=True)) + lam_init)

    def attend(n_keys):
        q = q_ref[...]
        k = k_ref[0:n_keys, :]
        v = v_ref[0:n_keys, :]
        parts = []
        for m in range(2):
            s = lax.dot_general(q[:, m * HEAD_DIM:(m + 1) * HEAD_DIM], k[:, m * HEAD_DIM:(m + 1) * HEAD_DIM],
                                (((1,), (1,)), ((), ())), preferred_element_type=F32)
            p = jnp.exp(s - jnp.max(s, axis=-1, keepdims=True))
            denom = jnp.sum(p, axis=-1, keepdims=True)
            parts.append(jnp.dot(p.astype(BF16), v, preferred_element_type=F32) / denom)
        o = parts[0] - lam * parts[1]
        o = o * lax.rsqrt(jnp.mean(o * o, axis=-1, keepdims=True) + SUBLN_EPS) * sub_ref[...]
        o_ref[...] = (o * (1.0 - lam_init)).astype(BF16)

    @pl.when(pl.program_id(2) == 0)
    def _():
        attend(CTX)

    @pl.when(pl.program_id(2) > 0)
    def _():
        attend(NZ)


def _attention(qkv, lam_vecs, subln, lam_init):
    return pl.pallas_call(
        functools.partial(_attn_kernel, lam_init=lam_init),
        grid=(B, HEADS, Q_TILES),
        in_specs=[pl.BlockSpec((4, HEAD_DIM), lambda b, h, t: (0, 0)),
                  pl.BlockSpec((TQ, V_DIM), lambda b, h, t: (b * Q_TILES + t, h)),
                  pl.BlockSpec((NZ, V_DIM), lambda b, h, t: (b, HEADS + h)),
                  pl.BlockSpec((NZ, V_DIM), lambda b, h, t: (b, 2 * HEADS + h)),
                  pl.BlockSpec((1, V_DIM), lambda b, h, t: (0, 0))],
        out_specs=pl.BlockSpec((TQ, V_DIM), lambda b, h, t: (b * Q_TILES + t, h)),
        out_shape=jax.ShapeDtypeStruct((T, HEADS * V_DIM), BF16),
        compiler_params=_cparams(("arbitrary", "arbitrary", "arbitrary")),
    )(lam_vecs, qkv, qkv, qkv, subln.reshape(1, V_DIM))


def _proj_res_kernel(a_ref, w_ref, x_ref, gb_ref, gc_ref, o_ref):
    acc = jnp.dot(a_ref[...], w_ref[...].astype(BF16), preferred_element_type=F32)
    gate = _pick(_is_ctx_rows(TM_MM), gb_ref, gc_ref)
    o_ref[...] = x_ref[...] + gate * acc


def _proj_res(a, w, xs, mods, gate_chunk):
    gb, gc = _mod_specs(gate_chunk, TM_MM, TN_MM)
    return pl.pallas_call(
        _proj_res_kernel,
        grid=(T // TM_MM, D // TN_MM),
        in_specs=[pl.BlockSpec((TM_MM, a.shape[1]), lambda i, j: (i, 0)),
                  pl.BlockSpec((a.shape[1], TN_MM), lambda i, j: (0, j)),
                  pl.BlockSpec((TM_MM, TN_MM), lambda i, j: (i, j)),
                  gb, gc],
        out_specs=pl.BlockSpec((TM_MM, TN_MM), lambda i, j: (i, j)),
        out_shape=jax.ShapeDtypeStruct((T, D), F32),
        compiler_params=_cparams(("arbitrary", "arbitrary")),
    )(a, w, xs, mods, mods)


def _discretize_kernel(are_ref, aim_ref, ldt_ref, bre_ref, bim_ref, abr_ref, abi_ref, bbr_ref, bbi_ref):
    a_re = jnp.minimum(are_ref[...], -1e-4)
    a_im = aim_ref[...]
    dt = jnp.exp(ldt_ref[...])
    mag = jnp.exp(a_re * dt)
    abar_re = mag * jnp.cos(a_im * dt)
    abar_im = mag * jnp.sin(a_im * dt)
    den = a_re * a_re + a_im * a_im
    f_re = ((abar_re - 1.0) * a_re + abar_im * a_im) / den
    f_im = (abar_im * a_re - (abar_re - 1.0) * a_im) / den
    b_re = bre_ref[...]
    b_im = bim_ref[...]
    abr_ref[...] = abar_re
    abi_ref[...] = abar_im
    bbr_ref[...] = f_re * b_re - f_im * b_im
    bbi_ref[...] = f_re * b_im + f_im * b_re


def _ssm_operators(a_re, a_im, log_dt, b_re, b_im, c_re, c_im):
    g, p, ch = SSM_GROUPS, SSM_STATE, SSM_CH
    rows, width = 2 * g, p * ch
    rep = lambda a: jnp.broadcast_to(a[..., None], (2, g, p, ch)).reshape(rows, width)
    spec = pl.BlockSpec((rows, width), lambda: (0, 0))
    abr, abi, bbr, bbi = pl.pallas_call(
        _discretize_kernel,
        in_specs=[spec] * 5,
        out_specs=[spec] * 4,
        out_shape=[jax.ShapeDtypeStruct((rows, width), F32)] * 4,
        compiler_params=pltpu.CompilerParams(vmem_limit_bytes=VMEM_LIMIT),
    )(rep(a_re), rep(a_im), rep(jnp.broadcast_to(log_dt[..., None], (2, g, p))),
      b_re.reshape(rows, width), b_im.reshape(rows, width))
    nj, gl = N_LANE_BLOCKS, GROUPS_PER_LANE_BLOCK

    def a_tiles(a):
        a = a.reshape(2, g, p, ch)[..., 0].reshape(2, nj, STATE_LANES).transpose(1, 0, 2)
        return jnp.repeat(a, B, axis=1)

    eye = jnp.eye(gl, dtype=F32)
    bb = jnp.stack([bbr, bbi]).reshape(2, 2, nj, gl, p, ch)
    w_drive = jnp.einsum('ab,rdjapc->jdacrbp', eye, bb).reshape(nj, 2 * LANE, 2 * STATE_LANES)
    cc = jnp.stack([c_re, -c_im]).reshape(2, 2, nj, gl, ch, p)
    w_read = jnp.einsum('ab,rdjacp->jrapdbc', eye, cc).reshape(nj, 2 * STATE_LANES, 2 * LANE)
    return a_tiles(abr), a_tiles(abi), w_drive.astype(BF16), w_read.astype(BF16)


def _scan_perm():
    perm = np.zeros((SCAN_ROWS, SCAN_ROWS), np.float32)
    for tau in range(SCAN_SUB):
        for s in range(SCAN_SEQS):
            src = tau if s < B else SCAN_SUB - 1 - tau
            perm[tau * SCAN_SEQS + s, s * SCAN_SUB + src] = 1.0
    return jnp.asarray(perm, BF16), jnp.asarray(perm.T, BF16)


def _prenorm_kernel(x_ref, nw_ref, shb_ref, shc_ref, scb_ref, scc_ref, o_ref):
    is_ctx = _is_ctx_rows(TM_EW)
    h = _norm_mod(x_ref[...], nw_ref[...], _pick(is_ctx, shb_ref, shc_ref), _pick(is_ctx, scb_ref, scc_ref))
    o_ref[...] = h.astype(o_ref.dtype)


def _prenorm(xs, mods, norm_w, shift_chunk, dtype):
    shb, shc = _mod_specs(shift_chunk, TM_EW)
    scb, scc = _mod_specs(shift_chunk + 1, TM_EW)
    return pl.pallas_call(
        _prenorm_kernel,
        grid=(T // TM_EW,),
        in_specs=[pl.BlockSpec((TM_EW, D), lambda i: (i, 0)),
                  pl.BlockSpec((1, D), lambda i: (0, 0)), shb, shc, scb, scc],
        out_specs=pl.BlockSpec((TM_EW, D), lambda i: (i, 0)),
        out_shape=jax.ShapeDtypeStruct((T, D), dtype),
        compiler_params=_cparams(("arbitrary",)),
    )(xs, norm_w.reshape(1, D), mods, mods, mods, mods)


SCAN_CHUNK = CTX
N_SCAN_CHUNKS = NZ // SCAN_CHUNK


def _bwd_chunk(ci):
    return jnp.where(ci == 0, 0, N_SCAN_CHUNKS - ci)


def _scan_kernel(hf_ref, hb_ref, are_ref, aim_ref, wd_ref, wr_ref, perm_ref, permt_ref,
                 yf_ref, yb_ref, sre_scr, sim_scr, bu_scr):
    @pl.when(pl.program_id(1) == 0)
    def _():
        sre_scr[...] = jnp.zeros_like(sre_scr)
        sim_scr[...] = jnp.zeros_like(sim_scr)

    a_re = are_ref[0]
    a_im = aim_ref[0]
    row = lax.broadcasted_iota(jnp.int32, (SCAN_ROWS, LANE), 0)
    is_fwd = (row % SCAN_SEQS) < B
    n_sub = SCAN_CHUNK // SCAN_SUB

    def sub_step(sub, carry):
        off_f = pl.multiple_of(sub * SCAN_SUB, SCAN_SUB)
        off_b = pl.multiple_of(SCAN_CHUNK - SCAN_SUB - sub * SCAN_SUB, SCAN_SUB)
        win = jnp.concatenate([hf_ref[:, pl.ds(off_f, SCAN_SUB), :].reshape(B * SCAN_SUB, LANE),
                               hb_ref[:, pl.ds(off_b, SCAN_SUB), :].reshape(B * SCAN_SUB, LANE)], axis=0)
        u = jnp.dot(perm_ref[...], win, preferred_element_type=F32)
        zero = jnp.zeros_like(u)
        lhs = jnp.concatenate([jnp.where(is_fwd, u, zero), jnp.where(is_fwd, zero, u)], axis=1).astype(BF16)
        bu_scr[...] = jnp.dot(lhs, wd_ref[0], preferred_element_type=F32)
        s_re = sre_scr[...]
        s_im = sim_scr[...]
        for tau in range(SCAN_SUB):
            rows = slice(tau * SCAN_SEQS, (tau + 1) * SCAN_SEQS)
            n_re = a_re * s_re - a_im * s_im + bu_scr[rows, 0:STATE_LANES]
            n_im = a_re * s_im + a_im * s_re + bu_scr[rows, STATE_LANES:2 * STATE_LANES]
            s_re, s_im = n_re, n_im
            bu_scr[rows, 0:STATE_LANES] = s_re
            bu_scr[rows, STATE_LANES:2 * STATE_LANES] = s_im
        sre_scr[...] = s_re
        sim_scr[...] = s_im
        y2 = jnp.dot(bu_scr[...].astype(BF16), wr_ref[0], preferred_element_type=F32)
        y = jnp.where(is_fwd, y2[:, 0:LANE], y2[:, LANE:2 * LANE])
        y_hi = y.astype(BF16)
        y_lo = (y - y_hi.astype(F32)).astype(BF16)
        yt = (jnp.dot(permt_ref[...], y_hi, preferred_element_type=F32)
              + jnp.dot(permt_ref[...], y_lo, preferred_element_type=F32))
        yf_ref[:, pl.ds(off_f, SCAN_SUB), :] = yt[0:B * SCAN_SUB].reshape(B, SCAN_SUB, LANE)
        yb_ref[:, pl.ds(off_b, SCAN_SUB), :] = yt[B * SCAN_SUB:].reshape(B, SCAN_SUB, LANE)
        return carry

    lax.fori_loop(0, n_sub, sub_step, 0)


def _ssm_scan(h, a_re_t, a_im_t, w_drive, w_read, perm, perm_t):
    h3 = h.reshape(B, NZ, D)
    blk = (B, SCAN_CHUNK, LANE)
    fwd_spec = pl.BlockSpec(blk, lambda j, ci: (0, ci, j))
    bwd_spec = pl.BlockSpec(blk, lambda j, ci: (0, _bwd_chunk(ci), j))
    a_spec = pl.BlockSpec((1, SCAN_SEQS, STATE_LANES), lambda j, ci: (j, 0, 0))
    p_spec = pl.BlockSpec((SCAN_ROWS, SCAN_ROWS), lambda j, ci: (0, 0))
    yf, yb = pl.pallas_call(
        _scan_kernel,
        grid=(N_LANE_BLOCKS, N_SCAN_CHUNKS),
        in_specs=[fwd_spec, bwd_spec, a_spec, a_spec,
                  pl.BlockSpec((1, 2 * LANE, 2 * STATE_LANES), lambda j, ci: (j, 0, 0)),
                  pl.BlockSpec((1, 2 * STATE_LANES, 2 * LANE), lambda j, ci: (j, 0, 0)),
                  p_spec, p_spec],
        out_specs=[fwd_spec, bwd_spec],
        out_shape=[jax.ShapeDtypeStruct((B, NZ, D), F32)] * 2,
        scratch_shapes=[pltpu.VMEM((SCAN_SEQS, STATE_LANES), F32),
                        pltpu.VMEM((SCAN_SEQS, STATE_LANES), F32),
                        pltpu.VMEM((SCAN_ROWS, 2 * STATE_LANES), F32)],
        compiler_params=_cparams(("arbitrary", "arbitrary")),
    )(h3, h3, a_re_t, a_im_t, w_drive, w_read, perm, perm_t)
    return yf.reshape(T, D), yb.reshape(T, D)


def _ssm_post_kernel(x_ref, nw_ref, shb_ref, shc_ref, scb_ref, scc_ref, d_ref, yf_ref, yb_ref, o_ref):
    is_ctx = _is_ctx_rows(TM_EW)
    h = _norm_mod(x_ref[...], nw_ref[...], _pick(is_ctx, shb_ref, shc_ref), _pick(is_ctx, scb_ref, scc_ref))
    y = d_ref[...] * h + yf_ref[...] + yb_ref[...]
    o_ref[...] = jax.nn.gelu(y).astype(BF16)


def _ssm_post(xs, mods, norm_w, d_skip, yf, yb):
    shb, shc = _mod_specs(0, TM_EW)
    scb, scc = _mod_specs(1, TM_EW)
    row = pl.BlockSpec((TM_EW, D), lambda i: (i, 0))
    vec = pl.BlockSpec((1, D), lambda i: (0, 0))
    return pl.pallas_call(
        _ssm_post_kernel,
        grid=(T // TM_EW,),
        in_specs=[row, vec, shb, shc, scb, scc, vec, row, row],
        out_specs=row,
        out_shape=jax.ShapeDtypeStruct((T, D), BF16),
        compiler_params=_cparams(("arbitrary",)),
    )(xs, norm_w.reshape(1, D), mods, mods, mods, mods, d_skip.reshape(1, D), yf, yb)


def _glu_res_kernel(a_ref, wa_ref, wb_ref, ba_ref, bb_ref, x_ref, gb_ref, gc_ref, o_ref):
    a = a_ref[...]
    za = jnp.dot(a, wa_ref[...].astype(BF16), preferred_element_type=F32) + ba_ref[...]
    zb = jnp.dot(a, wb_ref[...].astype(BF16), preferred_element_type=F32) + bb_ref[...]
    gate = _pick(_is_ctx_rows(TM_MM), gb_ref, gc_ref)
    o_ref[...] = x_ref[...] + gate * (za * jax.nn.sigmoid(zb))


def _glu_res(a, w_glu, b_glu, xs, mods):
    gb, gc = _mod_specs(2, TM_MM, TN_MM)
    nb = D // TN_MM
    return pl.pallas_call(
        _glu_res_kernel,
        grid=(T // TM_MM, nb),
        in_specs=[pl.BlockSpec((TM_MM, D), lambda i, j: (i, 0)),
                  pl.BlockSpec((D, TN_MM), lambda i, j: (0, j)),
                  pl.BlockSpec((D, TN_MM), lambda i, j: (0, j + nb)),
                  pl.BlockSpec((1, TN_MM), lambda i, j: (0, j)),
                  pl.BlockSpec((1, TN_MM), lambda i, j: (0, j + nb)),
                  pl.BlockSpec((TM_MM, TN_MM), lambda i, j: (i, j)),
                  gb, gc],
        out_specs=pl.BlockSpec((TM_MM, TN_MM), lambda i, j: (i, j)),
        out_shape=jax.ShapeDtypeStruct((T, D), F32),
        compiler_params=_cparams(("arbitrary", "arbitrary")),
    )(a, w_glu, w_glu, b_glu.reshape(1, 2 * D), b_glu.reshape(1, 2 * D), xs, mods, mods)


def _route_kernel(x_ref, nw_ref, shb_ref, shc_ref, scb_ref, scc_ref, wr_ref, br_ref,
                  h_ref, ri_ref, rw_ref, cnt_ref, carry_scr):
    @pl.when(pl.program_id(0) == 0)
    def _():
        carry_scr[...] = jnp.zeros_like(carry_scr)

    is_ctx = _is_ctx_rows(TM_RT)
    h = _norm_mod(x_ref[...], nw_ref[...], _pick(is_ctx, shb_ref, shc_ref), _pick(is_ctx, scb_ref, scc_ref))
    h_ref[...] = h

    w = wr_ref[...]
    h_hi = h.astype(BF16)
    h_lo = (h - h_hi.astype(F32)).astype(BF16)
    w_hi = w.astype(BF16)
    w_lo = (w - w_hi.astype(F32)).astype(BF16)
    logits = (jnp.dot(h_hi, w_hi, preferred_element_type=F32)
              + jnp.dot(h_hi, w_lo, preferred_element_type=F32)
              + jnp.dot(h_lo, w_hi, preferred_element_type=F32)) + br_ref[...]

    lane = lax.broadcasted_iota(jnp.int32, (TM_RT, ROUTE_LANES), 1).astype(F32)
    big = float(ROUTE_LANES)
    neg = -jnp.inf
    is_g = lane < N_GROUPS
    g_max = jnp.max(jnp.where(is_g, logits, neg), axis=-1, keepdims=True)
    g_sum = jnp.sum(jnp.where(is_g, jnp.exp(logits - g_max), 0.0), axis=-1, keepdims=True)
    g_p = 1.0 / g_sum
    g_idx = jnp.min(jnp.where(is_g, jnp.where(logits == g_max, lane, big), big), axis=-1, keepdims=True)
    lo = N_GROUPS + N_EPG * g_idx
    e_log = jnp.where(lane >= lo, jnp.where(lane < lo + N_EPG, logits, neg), neg)
    e1 = jnp.max(e_log, axis=-1, keepdims=True)
    i1 = jnp.min(jnp.where(e_log == e1, lane, big), axis=-1, keepdims=True)
    e_log2 = jnp.where(lane == i1, neg, e_log)
    e2 = jnp.max(e_log2, axis=-1, keepdims=True)
    i2 = jnp.min(jnp.where(e_log2 == e2, lane, big), axis=-1, keepdims=True)
    p2 = jnp.exp(e2 - e1)
    w1 = g_p / (1.0 + p2)
    w2 = g_p * p2 / (1.0 + p2)
    x1 = i1 - N_GROUPS
    x2 = i2 - N_GROUPS

    sel1 = lane == x1
    sel2 = lane == x2
    onehot = jnp.where(sel1, 1.0, jnp.where(sel2, 1.0, 0.0))
    r_i = lax.broadcasted_iota(jnp.int32, (TM_RT, TM_RT), 0)
    c_i = lax.broadcasted_iota(jnp.int32, (TM_RT, TM_RT), 1)
    tril = jnp.where(r_i > c_i, 1.0, 0.0).astype(BF16)
    before = jnp.dot(tril, onehot.astype(BF16), preferred_element_type=F32) + carry_scr[0:1, :]
    rank1 = jnp.sum(jnp.where(sel1, before, 0.0), axis=-1, keepdims=True)
    rank2 = jnp.sum(jnp.where(sel2, before, 0.0), axis=-1, keepdims=True)
    total = carry_scr[0:1, :] + jnp.sum(onehot, axis=0, keepdims=True)
    carry_scr[...] = jnp.broadcast_to(total, carry_scr.shape)
    cnt_ref[...] = jnp.broadcast_to(total, cnt_ref.shape)

    ri = jnp.where(lane == 0, x1, jnp.where(lane == 1, x2, jnp.where(lane == 2, rank1, jnp.where(lane == 3, rank2, 0.0))))
    ri_ref[...] = ri.astype(jnp.int32)
    rw_ref[...] = jnp.where(lane == 0, w1, jnp.where(lane == 1, w2, 0.0))


def _route(xs, mods, norm_w, w_rg, b_rg, w_re, b_re):
    pad = ROUTE_LANES - N_GROUPS - N_EXPERTS
    w_cat = jnp.concatenate([w_rg, w_re.reshape(D, N_EXPERTS), jnp.zeros((D, pad), F32)], axis=1)
    b_cat = jnp.concatenate([b_rg, b_re.reshape(N_EXPERTS), jnp.zeros((pad,), F32)]).reshape(1, ROUTE_LANES)
    shb, shc = _mod_specs(3, TM_RT)
    scb, scc = _mod_specs(4, TM_RT)
    lanes = pl.BlockSpec((TM_RT, ROUTE_LANES), lambda i: (i, 0))
    return pl.pallas_call(
        _route_kernel,
        grid=(T // TM_RT,),
        in_specs=[pl.BlockSpec((TM_RT, D), lambda i: (i, 0)),
                  pl.BlockSpec((1, D), lambda i: (0, 0)), shb, shc, scb, scc,
                  pl.BlockSpec((D, ROUTE_LANES), lambda i: (0, 0)),
                  pl.BlockSpec((1, ROUTE_LANES), lambda i: (0, 0))],
        out_specs=[pl.BlockSpec((TM_RT, D), lambda i: (i, 0)), lanes, lanes,
                   pl.BlockSpec((SUBLANE, ROUTE_LANES), lambda i: (0, 0))],
        out_shape=[jax.ShapeDtypeStruct((T, D), F32),
                   jax.ShapeDtypeStruct((T, ROUTE_LANES), jnp.int32),
                   jax.ShapeDtypeStruct((T, ROUTE_LANES), F32),
                   jax.ShapeDtypeStruct((SUBLANE, ROUTE_LANES), F32)],
        scratch_shapes=[pltpu.VMEM((SUBLANE, ROUTE_LANES), F32)],
        compiler_params=_cparams(("arbitrary",)),
    )(xs, norm_w.reshape(1, D), mods, mods, mods, mods, w_cat, b_cat)


def _row_copy(src_hbm, src_row, dst_vmem, dst_row, sem):
    return pltpu.make_async_copy(src_hbm.at[pl.ds(src_row, 1), :], dst_vmem.at[pl.ds(dst_row, 1), :], sem)


def _expert_kernel(be_ref, nv_ref, rt_ref, h_hbm, w1_ref, w3_ref, w2_ref, y_ref, x_scr, sem):
    blk = pl.program_id(0)

    @pl.when(blk < nv_ref[0])
    def _():
        base = blk * TE

        def start(r, c):
            _row_copy(h_hbm, rt_ref[base + r], x_scr, r, sem).start()
            return c

        def wait(r, c):
            _row_copy(h_hbm, 0, x_scr, r, sem).wait()
            return c

        lax.fori_loop(0, TE, start, 0)
        lax.fori_loop(0, TE, wait, 0)
        x = x_scr[...].astype(BF16)
        a = jnp.dot(x, w1_ref[0].astype(BF16), preferred_element_type=F32)
        c = jnp.dot(x, w3_ref[0].astype(BF16), preferred_element_type=F32)
        mid = (jax.nn.silu(a) * c).astype(BF16)
        y_ref[...] = jnp.dot(mid, w2_ref[0].astype(BF16), preferred_element_type=F32)

    @pl.when(blk >= nv_ref[0])
    def _():
        y_ref[...] = jnp.zeros_like(y_ref)


def _experts(h, block_expert, n_valid, row_token, w1, w3, w2):
    grid_spec = pltpu.PrefetchScalarGridSpec(
        num_scalar_prefetch=3,
        grid=(N_EBLOCKS,),
        in_specs=[pl.BlockSpec(memory_space=pl.ANY),
                  pl.BlockSpec((1, D, MOE_F), lambda b, be, nv, rt: (be[b], 0, 0)),
                  pl.BlockSpec((1, D, MOE_F), lambda b, be, nv, rt: (be[b], 0, 0)),
                  pl.BlockSpec((1, MOE_F, D), lambda b, be, nv, rt: (be[b], 0, 0))],
        out_specs=pl.BlockSpec((TE, D), lambda b, be, nv, rt: (jnp.minimum(b, nv[0]), 0)),
        scratch_shapes=[pltpu.VMEM((TE, D), F32), pltpu.SemaphoreType.DMA(())],
    )
    return pl.pallas_call(
        _expert_kernel,
        grid_spec=grid_spec,
        out_shape=jax.ShapeDtypeStruct((N_EROWS + TE, D), F32),
        compiler_params=_cparams(("arbitrary",)),
    )(block_expert, n_valid, row_token, h,
      w1.reshape(N_EXPERTS, D, MOE_F), w3.reshape(N_EXPERTS, D, MOE_F), w2.reshape(N_EXPERTS, MOE_F, D))


def _combine_kernel(dest_ref, y_hbm, rw_ref, x_ref, gb_ref, gc_ref, o_ref, buf, sem):
    base = pl.program_id(0) * TM_RT

    def start(r, c):
        for k in range(MOE_TOPK):
            _row_copy(y_hbm, dest_ref[(base + r) * MOE_TOPK + k], buf.at[k], r, sem).start()
        return c

    def wait(r, c):
        for k in range(MOE_TOPK):
            _row_copy(y_hbm, 0, buf.at[k], r, sem).wait()
        return c

    lax.fori_loop(0, TM_RT, start, 0)
    lax.fori_loop(0, TM_RT, wait, 0)
    rw = rw_ref[...]
    y = rw[:, 0:1] * buf[0] + rw[:, 1:2] * buf[1]
    gate = _pick(_is_ctx_rows(TM_RT), gb_ref, gc_ref)
    o_ref[...] = x_ref[...] + gate * y


def _combine(ys, dest, rw, xs, mods):
    gb, gc = _mod_specs(5, TM_RT)
    grid_spec = pltpu.PrefetchScalarGridSpec(
        num_scalar_prefetch=1,
        grid=(T // TM_RT,),
        in_specs=[pl.BlockSpec(memory_space=pl.ANY),
                  pl.BlockSpec((TM_RT, ROUTE_LANES), lambda i, d: (i, 0)),
                  pl.BlockSpec((TM_RT, D), lambda i, d: (i, 0)),
                  gb, gc],
        out_specs=pl.BlockSpec((TM_RT, D), lambda i, d: (i, 0)),
        scratch_shapes=[pltpu.VMEM((MOE_TOPK, TM_RT, D), F32), pltpu.SemaphoreType.DMA(())],
    )
    return pl.pallas_call(
        _combine_kernel,
        grid_spec=grid_spec,
        out_shape=jax.ShapeDtypeStruct((T, D), F32),
        compiler_params=_cparams(("arbitrary",)),
    )(dest, ys, rw, xs, mods, mods)


def _moe(xs, mods, norm_w, w_rg, b_rg, w_re, b_re, w1, w3, w2):
    h, ri, rw, cnt = _route(xs, mods, norm_w, w_rg, b_rg, w_re, b_re)
    counts = cnt[0, :N_EXPERTS].astype(jnp.int32)
    padded = (counts + TE - 1) // TE * TE
    pad_end = jnp.cumsum(padded)
    pad_start = pad_end - padded
    dest = (pad_start[ri[:, 0:MOE_TOPK]] + ri[:, MOE_TOPK:2 * MOE_TOPK]).reshape(-1)
    row_token = jnp.zeros((N_EROWS,), jnp.int32).at[dest].set(
        jnp.repeat(jnp.arange(T, dtype=jnp.int32), MOE_TOPK))
    n_valid = pad_end[-1] // TE
    blocks = jnp.arange(N_EBLOCKS, dtype=jnp.int32)
    block_expert = jnp.minimum(jnp.searchsorted(pad_end, jnp.minimum(blocks, n_valid - 1) * TE, side='right'),
                               N_EXPERTS - 1).astype(jnp.int32)
    ys = _experts(h, block_expert, n_valid.reshape(1).astype(jnp.int32), row_token, w1, w3, w2)
    return _combine(ys, dest.astype(jnp.int32), rw, xs, mods)


def kernel(x, c, ctx, c_ctx, ada_w, ada_b, norm1_w, norm2_w, attn_w_qkv, attn_q_norm, attn_k_norm, attn_lam_q1, attn_lam_k1, attn_lam_q2, attn_lam_k2, attn_subln, attn_w_o, ssm_a_re, ssm_a_im, ssm_log_dt, ssm_b_re, ssm_b_im, ssm_c_re, ssm_c_im, ssm_d, ssm_w_glu, ssm_b_glu, moe_w_rg, moe_b_rg, moe_w_re, moe_b_re, moe_w1, moe_w3, moe_w2):
    xs = jnp.concatenate([ctx, x], axis=1).reshape(T, D)
    mods_all = _ada_table(c, c_ctx, ada_w, ada_b)
    cos, sin = _rope_tables()
    perm, perm_t = _scan_perm()
    for i in range(DEPTH):
        j = i // 2
        mods = mods_all[i].reshape(MOD_ROWS * ADA_CHUNKS, 1, D)
        if i % 2 == 0:
            lam_init = 0.8 - 0.6 * math.exp(-0.3 * i)
            qkv = _qkv(xs, mods, norm1_w[i], attn_w_qkv[j], attn_q_norm[j], attn_k_norm[j], cos, sin)
            lam_vecs = jnp.stack([attn_lam_q1[j], attn_lam_k1[j], attn_lam_q2[j], attn_lam_k2[j]])
            o = _attention(qkv, lam_vecs, attn_subln[j], lam_init)
            xs = _proj_res(o, attn_w_o[j], xs, mods, 2)
        else:
            ops = _ssm_operators(ssm_a_re[j], ssm_a_im[j], ssm_log_dt[j], ssm_b_re[j], ssm_b_im[j],
                                 ssm_c_re[j], ssm_c_im[j])
            h = _prenorm(xs, mods, norm1_w[i], 0, BF16)
            yf, yb = _ssm_scan(h, *ops, perm, perm_t)
            g = _ssm_post(xs, mods, norm1_w[i], ssm_d[j], yf, yb)
            xs = _glu_res(g, ssm_w_glu[j], ssm_b_glu[j], xs, mods)
        xs = _moe(xs, mods, norm2_w[i], moe_w_rg[i], moe_b_rg[i], moe_w_re[i], moe_b_re[i],
                  moe_w1[i], moe_w3[i], moe_w2[i])
    return xs.reshape(B, NZ, D)[:, CTX:, :]
```

```python
import functools
import math

import jax
import jax.numpy as jnp
import numpy as np
from jax import lax
from jax.experimental import pallas as pl
from jax.experimental.pallas import tpu as pltpu

F32 = jnp.float32
BF16 = jnp.bfloat16

D = 2048
B = 4
SEQ = 2048
CTX = 256
NZ = CTX + SEQ
T = B * NZ
DEPTH = 4
GRID_W = 64
NORM_EPS = 1e-6
ADA_CHUNKS = 6
CTX_MOD_ROW = B
MOD_ROWS = 8

HEADS = 8
HEAD_DIM = 128
V_DIM = 2 * HEAD_DIM
QK_WIDTH = HEADS * 2 * HEAD_DIM
DA_SCALE = HEAD_DIM ** -0.5
SUBLN_EPS = 1e-5
ROPE_BASE = 10000.0

SSM_CH = 16
SSM_GROUPS = D // SSM_CH
SSM_STATE = 64
LANE = 128
SUBLANE = 8
GROUPS_PER_LANE_BLOCK = LANE // SSM_CH
N_LANE_BLOCKS = D // LANE
STATE_LANES = GROUPS_PER_LANE_BLOCK * SSM_STATE
SCAN_SEQS = 2 * B
SCAN_SUB = 32
SCAN_ROWS = SCAN_SUB * SCAN_SEQS

N_GROUPS = 4
N_EPG = 8
N_EXPERTS = N_GROUPS * N_EPG
MOE_F = 512
MOE_TOPK = 2
ROUTE_LANES = 128
TE = 256
N_EBLOCKS = (T * MOE_TOPK) // TE + N_EXPERTS
N_EROWS = N_EBLOCKS * TE
ROW_TILE = D // LANE
ROW_PITCH = 24

TM_MM = 1152
TN_MM = 512
NORM_ROWS = 128
TM_EW = 576
TM_RT = 256
VMEM_LIMIT = 56 * 1024 * 1024


def _cparams(sem):
    return pltpu.CompilerParams(dimension_semantics=sem, vmem_limit_bytes=VMEM_LIMIT)


def _mod_specs(chunk, tm, tn=None):
    if tn is None:
        return (pl.BlockSpec((1, 1, D), lambda i, *_: (((i * tm) // NZ) * ADA_CHUNKS + chunk, 0, 0)),
                pl.BlockSpec((1, 1, D), lambda i, *_: (CTX_MOD_ROW * ADA_CHUNKS + chunk, 0, 0)))
    return (pl.BlockSpec((1, 1, tn), lambda i, j: (((i * tm) // NZ) * ADA_CHUNKS + chunk, 0, j)),
            pl.BlockSpec((1, 1, tn), lambda i, j: (CTX_MOD_ROW * ADA_CHUNKS + chunk, 0, j)))


def _is_ctx_rows(tm):
    z0 = (pl.program_id(0) * tm) % NZ
    return (z0 + lax.broadcasted_iota(jnp.int32, (tm, 1), 0)) < CTX


def _pick(is_ctx, b_ref, c_ref):
    return jnp.where(is_ctx, c_ref[0], b_ref[0])


def _norm_mod(x, nw, sh, sc):
    y = x * lax.rsqrt(jnp.mean(x * x, axis=-1, keepdims=True) + NORM_EPS) * nw
    return y * (1.0 + sc) + sh


ADA_TN = 512
ADA_ROWS = B + 1


def _ada_kernel(ct_ref, w_ref, b_ref, o_ref, s_scr):
    @pl.when((pl.program_id(0) == 0) & (pl.program_id(1) == 0))
    def _():
        c = ct_ref[...]
        s = jax.nn.silu(c)
        for r in range(ADA_ROWS):
            s_scr[r] = jnp.broadcast_to(s[:, r:r + 1], (D, LANE))

    nj = ADA_TN // LANE

    def body(kb, accs):
        k0 = pl.multiple_of(kb * SUBLANE, SUBLANE)
        wk = w_ref[0, pl.ds(k0, SUBLANE), :]
        new = []
        for r in range(ADA_ROWS):
            sk = s_scr[r, pl.ds(k0, SUBLANE), :]
            for j in range(nj):
                new.append(accs[r * nj + j] + wk[:, j * LANE:(j + 1) * LANE] * sk)
        return tuple(new)

    zero = jnp.zeros((SUBLANE, LANE), F32)
    accs = lax.fori_loop(0, D // SUBLANE, body, (zero,) * (ADA_ROWS * nj), unroll=4)
    rows = []
    for r in range(ADA_ROWS):
        rows.append(jnp.concatenate(
            [jnp.sum(accs[r * nj + j], axis=0, keepdims=True) for j in range(nj)], axis=1))
    rows.append(jnp.zeros((MOD_ROWS - ADA_ROWS, ADA_TN), F32))
    o_ref[0] = jnp.concatenate(rows, axis=0) + b_ref[0]


def _ada_table(c, c_ctx, ada_w, ada_b):
    cs = jnp.concatenate([c, c_ctx[None, :], jnp.zeros((LANE - ADA_ROWS, D), F32)], axis=0)
    ct = cs.T
    n_out = ADA_CHUNKS * D
    return pl.pallas_call(
        _ada_kernel,
        grid=(DEPTH, n_out // ADA_TN),
        in_specs=[pl.BlockSpec((D, LANE), lambda l, j: (0, 0)),
                  pl.BlockSpec((1, D, ADA_TN), lambda l, j: (l, 0, j)),
                  pl.BlockSpec((1, 1, ADA_TN), lambda l, j: (l, 0, j))],
        out_specs=pl.BlockSpec((1, MOD_ROWS, ADA_TN), lambda l, j: (l, 0, j)),
        out_shape=jax.ShapeDtypeStruct((DEPTH, MOD_ROWS, n_out), F32),
        scratch_shapes=[pltpu.VMEM((ADA_ROWS, D, LANE), F32)],
        compiler_params=_cparams(("arbitrary", "arbitrary")),
    )(ct, ada_w, ada_b.reshape(DEPTH, 1, n_out))


def _rope_tables():
    half = HEAD_DIM // 4
    inv_freq = ROPE_BASE ** (-np.arange(half, dtype=np.float32) / half)
    t = np.arange(SEQ)
    row = (t // GRID_W).astype(np.float32)[:, None] * inv_freq[None, :]
    col = (t % GRID_W).astype(np.float32)[:, None] * inv_freq[None, :]
    cos_l = np.concatenate([np.cos(row), np.cos(row), np.cos(col), np.cos(col)], axis=1)
    sin_l = np.concatenate([-np.sin(row), np.sin(row), -np.sin(col), np.sin(col)], axis=1)
    cos = np.concatenate([np.ones((CTX, HEAD_DIM), np.float32), cos_l.astype(np.float32)], axis=0)
    sin = np.concatenate([np.zeros((CTX, HEAD_DIM), np.float32), sin_l.astype(np.float32)], axis=0)
    return jnp.asarray(cos), jnp.asarray(sin)


def _qkv_kernel(x_ref, nw_ref, shb_ref, shc_ref, scb_ref, scc_ref, w_ref, qn_ref, kn_ref,
                cos_ref, sin_ref, o_ref, h_scr):
    j = pl.program_id(1)

    @pl.when(j == 0)
    def _():
        z0 = (pl.program_id(0) * TM_MM) % NZ
        for r0 in range(0, TM_MM, NORM_ROWS):
            rows = slice(r0, r0 + NORM_ROWS)
            is_ctx = (z0 + r0 + lax.broadcasted_iota(jnp.int32, (NORM_ROWS, 1), 0)) < CTX
            h = _norm_mod(x_ref[rows, :], nw_ref[...], _pick(is_ctx, shb_ref, shc_ref),
                          _pick(is_ctx, scb_ref, scc_ref))
            h_scr[rows, :] = h.astype(BF16)

    acc = jnp.dot(h_scr[...], w_ref[...].astype(BF16), preferred_element_type=F32)
    n_q = QK_WIDTH // TN_MM

    @pl.when(j < 2 * n_q)
    def _():
        is_q = j < n_q
        nw = jnp.where(is_q, qn_ref[...], kn_ref[...])
        post = jnp.where(is_q, DA_SCALE, 1.0)
        cos = cos_ref[...]
        sin = sin_ref[...]
        lane = lax.broadcasted_iota(jnp.int32, (TM_MM, HEAD_DIM), 1)
        first_half = (lane % (HEAD_DIM // 2)) < (HEAD_DIM // 4)
        outs = []
        for c in range(TN_MM // HEAD_DIM):
            a = acc[:, c * HEAD_DIM:(c + 1) * HEAD_DIM]
            a = a * lax.rsqrt(jnp.mean(a * a, axis=-1, keepdims=True) + NORM_EPS) * nw
            up = pltpu.roll(a, HEAD_DIM - HEAD_DIM // 4, 1)
            dn = pltpu.roll(a, HEAD_DIM // 4, 1)
            a = a * cos + jnp.where(first_half, up, dn) * sin
            outs.append((a * post).astype(BF16))
        o_ref[...] = jnp.concatenate(outs, axis=1)

    @pl.when(j >= 2 * n_q)
    def _():
        o_ref[...] = acc.astype(BF16)


def _qkv(xs, mods, norm_w, w_qkv, layer, q_norm, k_norm, cos, sin):
    n_out = w_qkv.shape[2]
    shb, shc = _mod_specs(0, TM_MM)
    scb, scc = _mod_specs(1, TM_MM)
    tiles_per_batch = NZ // TM_MM
    return pl.pallas_call(
        _qkv_kernel,
        grid=(T // TM_MM, n_out // TN_MM),
        in_specs=[pl.BlockSpec((TM_MM, D), lambda i, j: (i, 0)),
                  pl.BlockSpec((1, D), lambda i, j: (0, 0)),
                  shb, shc, scb, scc,
                  pl.BlockSpec((None, D, TN_MM), lambda i, j: (layer, 0, j)),
                  pl.BlockSpec((1, HEAD_DIM), lambda i, j: (0, 0)),
                  pl.BlockSpec((1, HEAD_DIM), lambda i, j: (0, 0)),
                  pl.BlockSpec((TM_MM, HEAD_DIM), lambda i, j: (i % tiles_per_batch, 0)),
                  pl.BlockSpec((TM_MM, HEAD_DIM), lambda i, j: (i % tiles_per_batch, 0))],
        out_specs=pl.BlockSpec((TM_MM, TN_MM), lambda i, j: (i, j)),
        out_shape=jax.ShapeDtypeStruct((T, n_out), BF16),
        scratch_shapes=[pltpu.VMEM((TM_MM, D), BF16)],
        compiler_params=_cparams(("arbitrary", "arbitrary")),
    )(xs, norm_w.reshape(1, D), mods, mods, mods, mods, w_qkv,
      q_norm.reshape(1, HEAD_DIM), k_norm.reshape(1, HEAD_DIM), cos, sin)


TQ = 256
Q_TILES = NZ // TQ


def _attn_kernel(lam_ref, q_ref, k_ref, v_ref, sub_ref, o_ref, *, lam_init):
    lv = lam_ref[...]
    lam = (jnp.exp(jnp.sum(lv[0:1] * lv[1:2], axis=-1, keepdims=True))
           - jnp.exp(jnp.sum(lv[2:3] * lv[3:4], axis=-1, keepdims=True)) + lam_init)

    def attend(n_keys):
        q = q_ref[...]
        k = k_ref[0:n_keys, :]
        v = v_ref[0:n_keys, :]
        parts = []
        for m in range(2):
            s = lax.dot_general(q[:, m * HEAD_DIM:(m + 1) * HEAD_DIM], k[:, m * HEAD_DIM:(m + 1) * HEAD_DIM],
                                (((1,), (1,)), ((), ())), preferred_element_type=F32)
            p = jnp.exp(s - jnp.max(s, axis=-1, keepdims=True))
            denom = jnp.sum(p, axis=-1, keepdims=True)
            parts.append(jnp.dot(p.astype(BF16), v, preferred_element_type=F32) / denom)
        o = parts[0] - lam * parts[1]
        o = o * lax.rsqrt(jnp.mean(o * o, axis=-1, keepdims=True) + SUBLN_EPS) * sub_ref[...]
        o_ref[...] = (o * (1.0 - lam_init)).astype(BF16)

    @pl.when(pl.program_id(2) == 0)
    def _():
        attend(CTX)

    @pl.when(pl.program_id(2) > 0)
    def _():
        attend(NZ)


def _attention(qkv, lam_vecs, subln, lam_init):
    return pl.pallas_call(
        functools.partial(_attn_kernel, lam_init=lam_init),
        grid=(B, HEADS, Q_TILES),
        in_specs=[pl.BlockSpec((4, HEAD_DIM), lambda b, h, t: (0, 0)),
                  pl.BlockSpec((TQ, V_DIM), lambda b, h, t: (b * Q_TILES + t, h)),
                  pl.BlockSpec((NZ, V_DIM), lambda b, h, t: (b, HEADS + h)),
                  pl.BlockSpec((NZ, V_DIM), lambda b, h, t: (b, 2 * HEADS + h)),
                  pl.BlockSpec((1, V_DIM), lambda b, h, t: (0, 0))],
        out_specs=pl.BlockSpec((TQ, V_DIM), lambda b, h, t: (b * Q_TILES + t, h)),
        out_shape=jax.ShapeDtypeStruct((T, HEADS * V_DIM), BF16),
        compiler_params=_cparams(("arbitrary", "arbitrary", "arbitrary")),
    )(lam_vecs, qkv, qkv, qkv, subln.reshape(1, V_DIM))


def _proj_res_kernel(a_ref, w_ref, x_ref, gb_ref, gc_ref, o_ref):
    acc = jnp.dot(a_ref[...], w_ref[...].astype(BF16), preferred_element_type=F32)
    gate = _pick(_is_ctx_rows(TM_MM), gb_ref, gc_ref)
    o_ref[...] = x_ref[...] + gate * acc


def _proj_res(a, w, layer, xs, mods, gate_chunk):
    gb, gc = _mod_specs(gate_chunk, TM_MM, TN_MM)
    return pl.pallas_call(
        _proj_res_kernel,
        grid=(T // TM_MM, D // TN_MM),
        in_specs=[pl.BlockSpec((TM_MM, a.shape[1]), lambda i, j: (i, 0)),
                  pl.BlockSpec((None, a.shape[1], TN_MM), lambda i, j: (layer, 0, j)),
                  pl.BlockSpec((TM_MM, TN_MM), lambda i, j: (i, j)),
                  gb, gc],
        out_specs=pl.BlockSpec((TM_MM, TN_MM), lambda i, j: (i, j)),
        out_shape=jax.ShapeDtypeStruct((T, D), F32),
        compiler_params=_cparams(("arbitrary", "arbitrary")),
    )(a, w, xs, mods, mods)


def _discretize_kernel(are_ref, aim_ref, ldt_ref, bre_ref, bim_ref, abr_ref, abi_ref, bbr_ref, bbi_ref):
    a_re = jnp.minimum(are_ref[...], -1e-4)
    a_im = aim_ref[...]
    dt = jnp.exp(ldt_ref[...])
    mag = jnp.exp(a_re * dt)
    abar_re = mag * jnp.cos(a_im * dt)
    abar_im = mag * jnp.sin(a_im * dt)
    den = a_re * a_re + a_im * a_im
    f_re = ((abar_re - 1.0) * a_re + abar_im * a_im) / den
    f_im = (abar_im * a_re - (abar_re - 1.0) * a_im) / den
    b_re = bre_ref[...]
    b_im = bim_ref[...]
    abr_ref[...] = abar_re
    abi_ref[...] = abar_im
    bbr_ref[...] = f_re * b_re - f_im * b_im
    bbi_ref[...] = f_re * b_im + f_im * b_re


def _ssm_operators(a_re, a_im, log_dt, b_re, b_im, c_re, c_im):
    g, p, ch = SSM_GROUPS, SSM_STATE, SSM_CH
    rows, width = 2 * g, p * ch
    rep = lambda a: jnp.broadcast_to(a[..., None], (2, g, p, ch)).reshape(rows, width)
    spec = pl.BlockSpec((rows, width), lambda: (0, 0))
    abr, abi, bbr, bbi = pl.pallas_call(
        _discretize_kernel,
        in_specs=[spec] * 5,
        out_specs=[spec] * 4,
        out_shape=[jax.ShapeDtypeStruct((rows, width), F32)] * 4,
        compiler_params=pltpu.CompilerParams(vmem_limit_bytes=VMEM_LIMIT),
    )(rep(a_re), rep(a_im), rep(jnp.broadcast_to(log_dt[..., None], (2, g, p))),
      b_re.reshape(rows, width), b_im.reshape(rows, width))
    nj, gl = N_LANE_BLOCKS, GROUPS_PER_LANE_BLOCK

    def a_tiles(a):
        a = a.reshape(2, g, p, ch)[..., 0].reshape(2, nj, STATE_LANES).transpose(1, 0, 2)
        return jnp.repeat(a, B, axis=1)

    eye = jnp.eye(gl, dtype=F32)
    bb = jnp.stack([bbr, bbi]).reshape(2, 2, nj, gl, p, ch)
    w_drive = jnp.einsum('ab,rdjapc->jdacrbp', eye, bb).reshape(nj, 2 * LANE, 2 * STATE_LANES)
    cc = jnp.stack([c_re, -c_im]).reshape(2, 2, nj, gl, ch, p)
    w_read = jnp.einsum('ab,rdjacp->jrapdbc', eye, cc).reshape(nj, 2 * STATE_LANES, 2 * LANE)
    return a_tiles(abr), a_tiles(abi), w_drive.astype(BF16), w_read.astype(BF16)


def _scan_perm():
    perm = np.zeros((SCAN_ROWS, SCAN_ROWS), np.float32)
    for tau in range(SCAN_SUB):
        for s in range(SCAN_SEQS):
            src = tau if s < B else SCAN_SUB - 1 - tau
            perm[tau * SCAN_SEQS + s, s * SCAN_SUB + src] = 1.0
    return jnp.asarray(perm, BF16), jnp.asarray(perm.T, BF16)


def _prenorm_kernel(x_ref, nw_ref, shb_ref, shc_ref, scb_ref, scc_ref, o_ref):
    is_ctx = _is_ctx_rows(TM_EW)
    h = _norm_mod(x_ref[...], nw_ref[...], _pick(is_ctx, shb_ref, shc_ref), _pick(is_ctx, scb_ref, scc_ref))
    o_ref[...] = h.astype(o_ref.dtype)


def _prenorm(xs, mods, norm_w, shift_chunk, dtype):
    shb, shc = _mod_specs(shift_chunk, TM_EW)
    scb, scc = _mod_specs(shift_chunk + 1, TM_EW)
    return pl.pallas_call(
        _prenorm_kernel,
        grid=(T // TM_EW,),
        in_specs=[pl.BlockSpec((TM_EW, D), lambda i: (i, 0)),
                  pl.BlockSpec((1, D), lambda i: (0, 0)), shb, shc, scb, scc],
        out_specs=pl.BlockSpec((TM_EW, D), lambda i: (i, 0)),
        out_shape=jax.ShapeDtypeStruct((T, D), dtype),
        compiler_params=_cparams(("arbitrary",)),
    )(xs, norm_w.reshape(1, D), mods, mods, mods, mods)


SCAN_CHUNK = CTX
N_SCAN_CHUNKS = NZ // SCAN_CHUNK
SCAN_PARTS = 2
SUBS_PER_PART = SCAN_CHUNK // SCAN_SUB // SCAN_PARTS


def _bwd_chunk(ci):
    return jnp.where(ci == 0, 0, N_SCAN_CHUNKS - ci)


def _scan_kernel(hf_ref, hb_ref, are_ref, aim_ref, wd_ref, wr_ref, perm_ref, permt_ref,
                 yf_ref, yb_ref, sre_scr, sim_scr, bu_scr):
    @pl.when(pl.program_id(1) == 0)
    def _():
        sre_scr[...] = jnp.zeros_like(sre_scr)
        sim_scr[...] = jnp.zeros_like(sim_scr)

    a_re = are_ref[0]
    a_im = aim_ref[0]
    row = lax.broadcasted_iota(jnp.int32, (SCAN_ROWS, LANE), 0)
    is_fwd = (row % SCAN_SEQS) < B

    def windows(sub):
        off_f = sub * SCAN_SUB
        off_b = SCAN_CHUNK - SCAN_SUB - sub * SCAN_SUB
        return slice(off_f, off_f + SCAN_SUB), slice(off_b, off_b + SCAN_SUB)

    for part in range(SCAN_PARTS):
        lhs = []
        for q in range(SUBS_PER_PART):
            win_f, win_b = windows(part * SUBS_PER_PART + q)
            win = jnp.concatenate([hf_ref[:, win_f, :].reshape(B * SCAN_SUB, LANE),
                                   hb_ref[:, win_b, :].reshape(B * SCAN_SUB, LANE)], axis=0)
            u = jnp.dot(perm_ref[...], win, preferred_element_type=F32)
            zero = jnp.zeros_like(u)
            lhs.append(jnp.concatenate([jnp.where(is_fwd, u, zero), jnp.where(is_fwd, zero, u)],
                                       axis=1).astype(BF16))
        bu_scr[part] = jnp.dot(jnp.concatenate(lhs, axis=0), wd_ref[0], preferred_element_type=F32)

    s_re = sre_scr[...]
    s_im = sim_scr[...]
    for part in range(SCAN_PARTS):
        bu = bu_scr.at[part]
        for tau in range(SUBS_PER_PART * SCAN_SUB):
            rows = slice(tau * SCAN_SEQS, (tau + 1) * SCAN_SEQS)
            n_re = a_re * s_re - a_im * s_im + bu[rows, 0:STATE_LANES]
            n_im = a_re * s_im + a_im * s_re + bu[rows, STATE_LANES:2 * STATE_LANES]
            s_re, s_im = n_re, n_im
            bu[rows, 0:STATE_LANES] = s_re
            bu[rows, STATE_LANES:2 * STATE_LANES] = s_im
        y2 = jnp.dot(bu[...].astype(BF16), wr_ref[0], preferred_element_type=F32)
        for q in range(SUBS_PER_PART):
            win_f, win_b = windows(part * SUBS_PER_PART + q)
            yq = y2[q * SCAN_ROWS:(q + 1) * SCAN_ROWS]
            y = jnp.where(is_fwd, yq[:, 0:LANE], yq[:, LANE:2 * LANE])
            y_hi = y.astype(BF16)
            y_lo = (y - y_hi.astype(F32)).astype(BF16)
            yt = (jnp.dot(permt_ref[...], y_hi, preferred_element_type=F32)
                  + jnp.dot(permt_ref[...], y_lo, preferred_element_type=F32))
            yf_ref[:, win_f, :] = yt[0:B * SCAN_SUB].reshape(B, SCAN_SUB, LANE)
            yb_ref[:, win_b, :] = yt[B * SCAN_SUB:].reshape(B, SCAN_SUB, LANE)
    sre_scr[...] = s_re
    sim_scr[...] = s_im


def _ssm_scan(h, a_re_t, a_im_t, w_drive, w_read, perm, perm_t):
    h3 = h.reshape(B, NZ, D)
    blk = (B, SCAN_CHUNK, LANE)
    fwd_spec = pl.BlockSpec(blk, lambda j, ci: (0, ci, j))
    bwd_spec = pl.BlockSpec(blk, lambda j, ci: (0, _bwd_chunk(ci), j))
    a_spec = pl.BlockSpec((1, SCAN_SEQS, STATE_LANES), lambda j, ci: (j, 0, 0))
    p_spec = pl.BlockSpec((SCAN_ROWS, SCAN_ROWS), lambda j, ci: (0, 0))
    yf, yb = pl.pallas_call(
        _scan_kernel,
        grid=(N_LANE_BLOCKS, N_SCAN_CHUNKS),
        in_specs=[fwd_spec, bwd_spec, a_spec, a_spec,
                  pl.BlockSpec((1, 2 * LANE, 2 * STATE_LANES), lambda j, ci: (j, 0, 0)),
                  pl.BlockSpec((1, 2 * STATE_LANES, 2 * LANE), lambda j, ci: (j, 0, 0)),
                  p_spec, p_spec],
        out_specs=[fwd_spec, bwd_spec],
        out_shape=[jax.ShapeDtypeStruct((B, NZ, D), F32)] * 2,
        scratch_shapes=[pltpu.VMEM((SCAN_SEQS, STATE_LANES), F32),
                        pltpu.VMEM((SCAN_SEQS, STATE_LANES), F32),
                        pltpu.VMEM((SCAN_PARTS, SUBS_PER_PART * SCAN_ROWS, 2 * STATE_LANES), F32)],
        compiler_params=_cparams(("arbitrary", "arbitrary")),
    )(h3, h3, a_re_t, a_im_t, w_drive, w_read, perm, perm_t)
    return yf.reshape(T, D), yb.reshape(T, D)


def _ssm_post_kernel(x_ref, nw_ref, shb_ref, shc_ref, scb_ref, scc_ref, d_ref, yf_ref, yb_ref, o_ref):
    is_ctx = _is_ctx_rows(TM_EW)
    h = _norm_mod(x_ref[...], nw_ref[...], _pick(is_ctx, shb_ref, shc_ref), _pick(is_ctx, scb_ref, scc_ref))
    y = d_ref[...] * h + yf_ref[...] + yb_ref[...]
    o_ref[...] = jax.nn.gelu(y).astype(BF16)


def _ssm_post(xs, mods, norm_w, d_skip, yf, yb):
    shb, shc = _mod_specs(0, TM_EW)
    scb, scc = _mod_specs(1, TM_EW)
    row = pl.BlockSpec((TM_EW, D), lambda i: (i, 0))
    vec = pl.BlockSpec((1, D), lambda i: (0, 0))
    return pl.pallas_call(
        _ssm_post_kernel,
        grid=(T // TM_EW,),
        in_specs=[row, vec, shb, shc, scb, scc, vec, row, row],
        out_specs=row,
        out_shape=jax.ShapeDtypeStruct((T, D), BF16),
        compiler_params=_cparams(("arbitrary",)),
    )(xs, norm_w.reshape(1, D), mods, mods, mods, mods, d_skip.reshape(1, D), yf, yb)


def _glu_res_kernel(a_ref, wa_ref, wb_ref, ba_ref, bb_ref, x_ref, gb_ref, gc_ref, o_ref):
    a = a_ref[...]
    za = jnp.dot(a, wa_ref[...].astype(BF16), preferred_element_type=F32) + ba_ref[...]
    zb = jnp.dot(a, wb_ref[...].astype(BF16), preferred_element_type=F32) + bb_ref[...]
    gate = _pick(_is_ctx_rows(TM_MM), gb_ref, gc_ref)
    o_ref[...] = x_ref[...] + gate * (za * jax.nn.sigmoid(zb))


def _glu_res(a, w_glu, layer, b_glu, xs, mods):
    gb, gc = _mod_specs(2, TM_MM, TN_MM)
    nb = D // TN_MM
    return pl.pallas_call(
        _glu_res_kernel,
        grid=(T // TM_MM, nb),
        in_specs=[pl.BlockSpec((TM_MM, D), lambda i, j: (i, 0)),
                  pl.BlockSpec((None, D, TN_MM), lambda i, j: (layer, 0, j)),
                  pl.BlockSpec((None, D, TN_MM), lambda i, j: (layer, 0, j + nb)),
                  pl.BlockSpec((1, TN_MM), lambda i, j: (0, j)),
                  pl.BlockSpec((1, TN_MM), lambda i, j: (0, j + nb)),
                  pl.BlockSpec((TM_MM, TN_MM), lambda i, j: (i, j)),
                  gb, gc],
        out_specs=pl.BlockSpec((TM_MM, TN_MM), lambda i, j: (i, j)),
        out_shape=jax.ShapeDtypeStruct((T, D), F32),
        compiler_params=_cparams(("arbitrary", "arbitrary")),
    )(a, w_glu, w_glu, b_glu.reshape(1, 2 * D), b_glu.reshape(1, 2 * D), xs, mods, mods)


def _tok_rows_load(ref, n):
    return jnp.concatenate([ref[pl.ds(j, n, stride=ROW_PITCH), :] for j in range(ROW_TILE)], axis=1)


def _tok_rows_store(ref, val, n):
    for j in range(ROW_TILE):
        ref[pl.ds(j, n, stride=ROW_TILE), :] = val[:, j * LANE:(j + 1) * LANE]


def _route_kernel(x_ref, nw_ref, shb_ref, shc_ref, scb_ref, scc_ref, wr_ref, br_ref,
                  h_ref, ri_ref, rw_ref, cnt_ref, carry_scr):
    @pl.when(pl.program_id(0) == 0)
    def _():
        carry_scr[...] = jnp.zeros_like(carry_scr)

    is_ctx = _is_ctx_rows(TM_RT)
    h = _norm_mod(x_ref[...], nw_ref[...], _pick(is_ctx, shb_ref, shc_ref), _pick(is_ctx, scb_ref, scc_ref))
    _tok_rows_store(h_ref, h, TM_RT)

    w = wr_ref[...]
    h_hi = h.astype(BF16)
    h_lo = (h - h_hi.astype(F32)).astype(BF16)
    w_hi = w.astype(BF16)
    w_lo = (w - w_hi.astype(F32)).astype(BF16)
    logits = (jnp.dot(h_hi, w_hi, preferred_element_type=F32)
              + jnp.dot(h_hi, w_lo, preferred_element_type=F32)
              + jnp.dot(h_lo, w_hi, preferred_element_type=F32)) + br_ref[...]

    lane = lax.broadcasted_iota(jnp.int32, (TM_RT, ROUTE_LANES), 1).astype(F32)
    big = float(ROUTE_LANES)
    neg = -jnp.inf
    is_g = lane < N_GROUPS
    g_max = jnp.max(jnp.where(is_g, logits, neg), axis=-1, keepdims=True)
    g_sum = jnp.sum(jnp.where(is_g, jnp.exp(logits - g_max), 0.0), axis=-1, keepdims=True)
    g_p = 1.0 / g_sum
    g_idx = jnp.min(jnp.where(is_g, jnp.where(logits == g_max, lane, big), big), axis=-1, keepdims=True)
    lo = N_GROUPS + N_EPG * g_idx
    e_log = jnp.where(lane >= lo, jnp.where(lane < lo + N_EPG, logits, neg), neg)
    e1 = jnp.max(e_log, axis=-1, keepdims=True)
    i1 = jnp.min(jnp.where(e_log == e1, lane, big), axis=-1, keepdims=True)
    e_log2 = jnp.where(lane == i1, neg, e_log)
    e2 = jnp.max(e_log2, axis=-1, keepdims=True)
    i2 = jnp.min(jnp.where(e_log2 == e2, lane, big), axis=-1, keepdims=True)
    p2 = jnp.exp(e2 - e1)
    w1 = g_p / (1.0 + p2)
    w2 = g_p * p2 / (1.0 + p2)
    x1 = i1 - N_GROUPS
    x2 = i2 - N_GROUPS

    sel1 = lane == x1
    sel2 = lane == x2
    onehot = jnp.where(sel1, 1.0, jnp.where(sel2, 1.0, 0.0))
    r_i = lax.broadcasted_iota(jnp.int32, (TM_RT, TM_RT), 0)
    c_i = lax.broadcasted_iota(jnp.int32, (TM_RT, TM_RT), 1)
    tril = jnp.where(r_i > c_i, 1.0, 0.0).astype(BF16)
    before = jnp.dot(tril, onehot.astype(BF16), preferred_element_type=F32) + carry_scr[0:1, :]
    rank1 = jnp.sum(jnp.where(sel1, before, 0.0), axis=-1, keepdims=True)
    rank2 = jnp.sum(jnp.where(sel2, before, 0.0), axis=-1, keepdims=True)
    total = carry_scr[0:1, :] + jnp.sum(onehot, axis=0, keepdims=True)
    carry_scr[...] = jnp.broadcast_to(total, carry_scr.shape)
    cnt_ref[...] = jnp.broadcast_to(total, cnt_ref.shape)

    ri = jnp.where(lane == 0, x1, jnp.where(lane == 1, x2, jnp.where(lane == 2, rank1, jnp.where(lane == 3, rank2, 0.0))))
    ri_ref[...] = ri.astype(jnp.int32)
    rw_ref[...] = jnp.where(lane == 0, w1, jnp.where(lane == 1, w2, 0.0))


def _route(xs, mods, norm_w, w_rg, b_rg, w_re, b_re):
    pad = ROUTE_LANES - N_GROUPS - N_EXPERTS
    w_cat = jnp.concatenate([w_rg, w_re.reshape(D, N_EXPERTS), jnp.zeros((D, pad), F32)], axis=1)
    b_cat = jnp.concatenate([b_rg, b_re.reshape(N_EXPERTS), jnp.zeros((pad,), F32)]).reshape(1, ROUTE_LANES)
    shb, shc = _mod_specs(3, TM_RT)
    scb, scc = _mod_specs(4, TM_RT)
    lanes = pl.BlockSpec((TM_RT, ROUTE_LANES), lambda i: (i, 0))
    return pl.pallas_call(
        _route_kernel,
        grid=(T // TM_RT,),
        in_specs=[pl.BlockSpec((TM_RT, D), lambda i: (i, 0)),
                  pl.BlockSpec((1, D), lambda i: (0, 0)), shb, shc, scb, scc,
                  pl.BlockSpec((D, ROUTE_LANES), lambda i: (0, 0)),
                  pl.BlockSpec((1, ROUTE_LANES), lambda i: (0, 0))],
        out_specs=[pl.BlockSpec((TM_RT * ROW_TILE, LANE), lambda i: (i, 0)), lanes, lanes,
                   pl.BlockSpec((SUBLANE, ROUTE_LANES), lambda i: (0, 0))],
        out_shape=[jax.ShapeDtypeStruct((T * ROW_TILE, LANE), F32),
                   jax.ShapeDtypeStruct((T, ROUTE_LANES), jnp.int32),
                   jax.ShapeDtypeStruct((T, ROUTE_LANES), F32),
                   jax.ShapeDtypeStruct((SUBLANE, ROUTE_LANES), F32)],
        scratch_shapes=[pltpu.VMEM((SUBLANE, ROUTE_LANES), F32)],
        compiler_params=_cparams(("arbitrary",)),
    )(xs, norm_w.reshape(1, D), mods, mods, mods, mods, w_cat, b_cat)


def _row_gather(src_hbm, off_ref, off_index, dst, sem, n, wait):
    def body(r, c):
        off = 0 if wait else pl.multiple_of(off_ref[off_index(r)], ROW_TILE)
        cp = pltpu.make_async_copy(src_hbm.at[pl.ds(off, ROW_TILE), :],
                                   dst.at[pl.ds(pl.multiple_of(r * ROW_PITCH, SUBLANE), ROW_TILE), :], sem)
        if wait:
            cp.wait()
        else:
            cp.start()
        return c

    lax.fori_loop(0, n, body, 0, unroll=8)


def _expert_kernel(be_ref, nv_ref, ro_ref, h_hbm, w1_ref, w3_ref, w2_ref, y_ref, x_scr, sems):
    blk = pl.program_id(0)
    n_valid = nv_ref[0]
    slot = blk % 2

    def gather(block, into, wait):
        _row_gather(h_hbm, ro_ref, lambda r: block * TE + r, x_scr.at[into], sems.at[into], TE, wait)

    @pl.when(blk == 0)
    def _():
        gather(0, 0, False)

    @pl.when(blk + 1 < n_valid)
    def _():
        gather(blk + 1, 1 - slot, False)

    @pl.when(blk < n_valid)
    def _():
        gather(blk, slot, True)
        x = _tok_rows_load(x_scr.at[slot], TE).astype(BF16)
        a = jnp.dot(x, w1_ref[0].astype(BF16), preferred_element_type=F32)
        c = jnp.dot(x, w3_ref[0].astype(BF16), preferred_element_type=F32)
        mid = (jax.nn.silu(a) * c).astype(BF16)
        _tok_rows_store(y_ref, jnp.dot(mid, w2_ref[0].astype(BF16), preferred_element_type=F32), TE)

    @pl.when(blk >= n_valid)
    def _():
        y_ref[...] = jnp.zeros_like(y_ref)


def _experts(h_rows, block_expert, n_valid, row_off, w1, w3, w2):
    grid_spec = pltpu.PrefetchScalarGridSpec(
        num_scalar_prefetch=3,
        grid=(N_EBLOCKS,),
        in_specs=[pl.BlockSpec(memory_space=pl.ANY),
                  pl.BlockSpec((1, D, MOE_F), lambda b, be, nv, ro: (be[b], 0, 0)),
                  pl.BlockSpec((1, D, MOE_F), lambda b, be, nv, ro: (be[b], 0, 0)),
                  pl.BlockSpec((1, MOE_F, D), lambda b, be, nv, ro: (be[b], 0, 0))],
        out_specs=pl.BlockSpec((TE * ROW_TILE, LANE), lambda b, be, nv, ro: (jnp.minimum(b, nv[0]), 0)),
        scratch_shapes=[pltpu.VMEM((2, TE * ROW_PITCH, LANE), F32), pltpu.SemaphoreType.DMA((2,))],
    )
    return pl.pallas_call(
        _expert_kernel,
        grid_spec=grid_spec,
        out_shape=jax.ShapeDtypeStruct((N_EROWS * ROW_TILE, LANE), F32),
        compiler_params=_cparams(("arbitrary",)),
    )(block_expert, n_valid, row_off, h_rows,
      w1.reshape(DEPTH * N_EXPERTS, D, MOE_F), w3.reshape(DEPTH * N_EXPERTS, D, MOE_F),
      w2.reshape(DEPTH * N_EXPERTS, MOE_F, D))


def _combine_kernel(do_ref, y_hbm, rw_ref, x_ref, gb_ref, gc_ref, o_ref, buf, sems):
    tile = pl.program_id(0)
    slot = tile % 2

    def gather(t, into, wait):
        for k in range(MOE_TOPK):
            _row_gather(y_hbm, do_ref, lambda r: (t * TM_RT + r) * MOE_TOPK + k, buf.at[into, k],
                        sems.at[into], TM_RT, wait)

    @pl.when(tile == 0)
    def _():
        gather(0, 0, False)

    @pl.when(tile + 1 < pl.num_programs(0))
    def _():
        gather(tile + 1, 1 - slot, False)

    gather(tile, slot, True)
    rw = rw_ref[...]
    y = (rw[:, 0:1] * _tok_rows_load(buf.at[slot, 0], TM_RT)
         + rw[:, 1:2] * _tok_rows_load(buf.at[slot, 1], TM_RT))
    gate = _pick(_is_ctx_rows(TM_RT), gb_ref, gc_ref)
    o_ref[...] = x_ref[...] + gate * y


def _combine(ys_rows, dest_off, rw, xs, mods):
    gb, gc = _mod_specs(5, TM_RT)
    grid_spec = pltpu.PrefetchScalarGridSpec(
        num_scalar_prefetch=1,
        grid=(T // TM_RT,),
        in_specs=[pl.BlockSpec(memory_space=pl.ANY),
                  pl.BlockSpec((TM_RT, ROUTE_LANES), lambda i, d: (i, 0)),
                  pl.BlockSpec((TM_RT, D), lambda i, d: (i, 0)),
                  gb, gc],
        out_specs=pl.BlockSpec((TM_RT, D), lambda i, d: (i, 0)),
        scratch_shapes=[pltpu.VMEM((2, MOE_TOPK, TM_RT * ROW_PITCH, LANE), F32), pltpu.SemaphoreType.DMA((2,))],
    )
    return pl.pallas_call(
        _combine_kernel,
        grid_spec=grid_spec,
        out_shape=jax.ShapeDtypeStruct((T, D), F32),
        compiler_params=_cparams(("arbitrary",)),
    )(dest_off, ys_rows, rw, xs, mods, mods)


def _moe(xs, mods, layer, norm_w, w_rg, b_rg, w_re, b_re, w1, w3, w2):
    h_rows, ri, rw, cnt = _route(xs, mods, norm_w, w_rg, b_rg, w_re, b_re)
    counts = cnt[0, :N_EXPERTS].astype(jnp.int32)
    padded = (counts + TE - 1) // TE * TE
    pad_end = jnp.cumsum(padded)
    pad_start = pad_end - padded
    dest = (pad_start[ri[:, 0:MOE_TOPK]] + ri[:, MOE_TOPK:2 * MOE_TOPK]).reshape(-1)
    row_off = jnp.zeros((N_EROWS,), jnp.int32).at[dest].set(
        jnp.repeat(jnp.arange(T, dtype=jnp.int32) * ROW_TILE, MOE_TOPK))
    n_valid = pad_end[-1] // TE
    first_row = jnp.minimum(jnp.arange(N_EBLOCKS, dtype=jnp.int32), n_valid - 1) * TE
    block_expert = jnp.sum(pad_end[None, :] <= first_row[:, None], axis=1).astype(jnp.int32)
    block_expert = jnp.minimum(block_expert, N_EXPERTS - 1) + layer * N_EXPERTS
    ys_rows = _experts(h_rows, block_expert, n_valid.reshape(1).astype(jnp.int32), row_off, w1, w3, w2)
    return _combine(ys_rows, (dest * ROW_TILE).astype(jnp.int32), rw, xs, mods)


def kernel(x, c, ctx, c_ctx, ada_w, ada_b, norm1_w, norm2_w, attn_w_qkv, attn_q_norm, attn_k_norm, attn_lam_q1, attn_lam_k1, attn_lam_q2, attn_lam_k2, attn_subln, attn_w_o, ssm_a_re, ssm_a_im, ssm_log_dt, ssm_b_re, ssm_b_im, ssm_c_re, ssm_c_im, ssm_d, ssm_w_glu, ssm_b_glu, moe_w_rg, moe_b_rg, moe_w_re, moe_b_re, moe_w1, moe_w3, moe_w2):
    xs = jnp.concatenate([ctx, x], axis=1).reshape(T, D)
    mods_all = _ada_table(c, c_ctx, ada_w, ada_b)
    cos, sin = _rope_tables()
    perm, perm_t = _scan_perm()
    for i in range(DEPTH):
        j = i // 2
        mods = mods_all[i].reshape(MOD_ROWS * ADA_CHUNKS, 1, D)
        if i % 2 == 0:
            lam_init = 0.8 - 0.6 * math.exp(-0.3 * i)
            qkv = _qkv(xs, mods, norm1_w[i], attn_w_qkv, j, attn_q_norm[j], attn_k_norm[j], cos, sin)
            lam_vecs = jnp.stack([attn_lam_q1[j], attn_lam_k1[j], attn_lam_q2[j], attn_lam_k2[j]])
            o = _attention(qkv, lam_vecs, attn_subln[j], lam_init)
            xs = _proj_res(o, attn_w_o, j, xs, mods, 2)
        else:
            ops = _ssm_operators(ssm_a_re[j], ssm_a_im[j], ssm_log_dt[j], ssm_b_re[j], ssm_b_im[j],
                                 ssm_c_re[j], ssm_c_im[j])
            h = _prenorm(xs, mods, norm1_w[i], 0, BF16)
            yf, yb = _ssm_scan(h, *ops, perm, perm_t)
            g = _ssm_post(xs, mods, norm1_w[i], ssm_d[j], yf, yb)
            xs = _glu_res(g, ssm_w_glu, j, ssm_b_glu[j], xs, mods)
        xs = _moe(xs, mods, i, norm2_w[i], moe_w_rg[i], moe_b_rg[i], moe_w_re[i], moe_b_re[i],
                  moe_w1, moe_w3, moe_w2)
    return xs.reshape(B, NZ, D)[:, CTX:, :]
```

```python
import functools
import math

import jax
import jax.numpy as jnp
import numpy as np
from jax import lax
from jax.experimental import pallas as pl
from jax.experimental.pallas import tpu as pltpu

F32 = jnp.float32
BF16 = jnp.bfloat16

D = 2048
B = 4
SEQ = 2048
CTX = 256
NZ = CTX + SEQ
T = B * NZ
DEPTH = 4
GRID_W = 64
NORM_EPS = 1e-6
ADA_CHUNKS = 6
CTX_MOD_ROW = B
MOD_ROWS = 8

HEADS = 8
HEAD_DIM = 128
V_DIM = 2 * HEAD_DIM
QK_WIDTH = HEADS * 2 * HEAD_DIM
DA_SCALE = HEAD_DIM ** -0.5
Q_SCALE = DA_SCALE * math.log2(math.e)
SUBLN_EPS = 1e-5
ROPE_BASE = 10000.0

SSM_CH = 16
SSM_GROUPS = D // SSM_CH
SSM_STATE = 64
LANE = 128
SUBLANE = 8
GROUPS_PER_LANE_BLOCK = LANE // SSM_CH
N_LANE_BLOCKS = D // LANE
STATE_LANES = GROUPS_PER_LANE_BLOCK * SSM_STATE
SCAN_SEQS = 2 * B
SCAN_SUB = 32
SCAN_ROWS = SCAN_SUB * SCAN_SEQS

N_GROUPS = 4
N_EPG = 8
N_EXPERTS = N_GROUPS * N_EPG
MOE_F = 512
MOE_TOPK = 2
ROUTE_LANES = 128
TE = 256
N_EBLOCKS = (T * MOE_TOPK) // TE + N_EXPERTS
N_EROWS = N_EBLOCKS * TE
ROW_TILE = D // LANE
ROW_PITCH = 24
GATHER_UNROLL = 8

TM_MM = 1152
TN_MM = 512
NORM_ROWS = 32
TM_EW = 576
TM_RT = 256
VMEM_LIMIT = 56 * 1024 * 1024


def _cparams(sem):
    return pltpu.CompilerParams(dimension_semantics=sem, vmem_limit_bytes=VMEM_LIMIT)


def _mod_specs(chunk, tm, tn=None):
    if tn is None:
        return (pl.BlockSpec((1, 1, D), lambda i, *_: (((i * tm) // NZ) * ADA_CHUNKS + chunk, 0, 0)),
                pl.BlockSpec((1, 1, D), lambda i, *_: (CTX_MOD_ROW * ADA_CHUNKS + chunk, 0, 0)))
    return (pl.BlockSpec((1, 1, tn), lambda i, j: (((i * tm) // NZ) * ADA_CHUNKS + chunk, 0, j)),
            pl.BlockSpec((1, 1, tn), lambda i, j: (CTX_MOD_ROW * ADA_CHUNKS + chunk, 0, j)))


def _is_ctx_rows(tm):
    z0 = (pl.program_id(0) * tm) % NZ
    return (z0 + lax.broadcasted_iota(jnp.int32, (tm, 1), 0)) < CTX


def _pick(is_ctx, b_ref, c_ref):
    return jnp.where(is_ctx, c_ref[0], b_ref[0])


def _norm_mod(x, nw, sh, sc):
    y = x * lax.rsqrt(jnp.mean(x * x, axis=-1, keepdims=True) + NORM_EPS) * nw
    return y * (1.0 + sc) + sh


ADA_TN = 512
ADA_ROWS = B + 1


def _ada_kernel(ct_ref, w_ref, b_ref, o_ref, s_scr):
    @pl.when((pl.program_id(0) == 0) & (pl.program_id(1) == 0))
    def _():
        c = ct_ref[...]
        s = jax.nn.silu(c)
        for r in range(ADA_ROWS):
            s_scr[r] = jnp.broadcast_to(s[:, r:r + 1], (D, LANE))

    nj = ADA_TN // LANE

    def body(kb, accs):
        k0 = pl.multiple_of(kb * SUBLANE, SUBLANE)
        wk = w_ref[0, pl.ds(k0, SUBLANE), :]
        new = []
        for r in range(ADA_ROWS):
            sk = s_scr[r, pl.ds(k0, SUBLANE), :]
            for j in range(nj):
                new.append(accs[r * nj + j] + wk[:, j * LANE:(j + 1) * LANE] * sk)
        return tuple(new)

    zero = jnp.zeros((SUBLANE, LANE), F32)
    accs = lax.fori_loop(0, D // SUBLANE, body, (zero,) * (ADA_ROWS * nj), unroll=4)
    rows = []
    for r in range(ADA_ROWS):
        rows.append(jnp.concatenate(
            [jnp.sum(accs[r * nj + j], axis=0, keepdims=True) for j in range(nj)], axis=1))
    rows.append(jnp.zeros((MOD_ROWS - ADA_ROWS, ADA_TN), F32))
    o_ref[0] = jnp.concatenate(rows, axis=0) + b_ref[0]


def _ada_table(c, c_ctx, ada_w, ada_b):
    cs = jnp.concatenate([c, c_ctx[None, :], jnp.zeros((LANE - ADA_ROWS, D), F32)], axis=0)
    ct = cs.T
    n_out = ADA_CHUNKS * D
    return pl.pallas_call(
        _ada_kernel,
        grid=(DEPTH, n_out // ADA_TN),
        in_specs=[pl.BlockSpec((D, LANE), lambda l, j: (0, 0)),
                  pl.BlockSpec((1, D, ADA_TN), lambda l, j: (l, 0, j)),
                  pl.BlockSpec((1, 1, ADA_TN), lambda l, j: (l, 0, j))],
        out_specs=pl.BlockSpec((1, MOD_ROWS, ADA_TN), lambda l, j: (l, 0, j)),
        out_shape=jax.ShapeDtypeStruct((DEPTH, MOD_ROWS, n_out), F32),
        scratch_shapes=[pltpu.VMEM((ADA_ROWS, D, LANE), F32)],
        compiler_params=_cparams(("arbitrary", "arbitrary")),
    )(ct, ada_w, ada_b.reshape(DEPTH, 1, n_out))


def _rope_tables():
    half = HEAD_DIM // 4
    inv_freq = ROPE_BASE ** (-np.arange(half, dtype=np.float32) / half)
    t = np.arange(SEQ)
    row = (t // GRID_W).astype(np.float32)[:, None] * inv_freq[None, :]
    col = (t % GRID_W).astype(np.float32)[:, None] * inv_freq[None, :]
    cos_l = np.concatenate([np.cos(row), np.cos(row), np.cos(col), np.cos(col)], axis=1)
    sin_l = np.concatenate([-np.sin(row), np.sin(row), -np.sin(col), np.sin(col)], axis=1)
    cos = np.concatenate([np.ones((CTX, HEAD_DIM), np.float32), cos_l.astype(np.float32)], axis=0)
    sin = np.concatenate([np.zeros((CTX, HEAD_DIM), np.float32), sin_l.astype(np.float32)], axis=0)
    return jnp.asarray(cos), jnp.asarray(sin)


TM_QKV = 768
TN_QKV = 1024
TN_HALF = TN_QKV // 2
EP_ROWS = 64


_ROT_PARTNER = np.arange(HEAD_DIM) ^ (HEAD_DIM // 4)


def _head_lane_matrices():
    lanes = np.arange(2 * HEAD_DIM)
    partner = (lanes // HEAD_DIM) * HEAD_DIM + _ROT_PARTNER[lanes % HEAD_DIM]
    swap = (lanes[:, None] == partner[None, :]).astype(np.float32)
    ones = (lanes[:, None] // HEAD_DIM == lanes[None, :] // HEAD_DIM).astype(np.float32)
    return jnp.asarray(swap, BF16), jnp.asarray(ones, BF16)


def _qkv_prologue(x_ref, nw_ref, shb_ref, shc_ref, scb_ref, scc_ref, h_scr):
    @pl.when(pl.program_id(1) == 0)
    def _():
        z0 = (pl.program_id(0) * TM_QKV) % NZ
        for r0 in range(0, TM_QKV, NORM_ROWS):
            rows = slice(r0, r0 + NORM_ROWS)
            is_ctx = z0 + r0 < CTX
            h = _norm_mod(x_ref[rows, :], nw_ref[...], _pick(is_ctx, shb_ref, shc_ref),
                          _pick(is_ctx, scb_ref, scc_ref))
            h_scr[rows, :] = h.astype(BF16)


def _qk_kernel(x_ref, nw_ref, shb_ref, shc_ref, scb_ref, scc_ref, w_ref, qn_ref, kn_ref,
               cos_ref, sin_ref, swap_ref, ones_ref, o_ref, h_scr, acc_scr, rot_scr, ssq_scr):
    _qkv_prologue(x_ref, nw_ref, shb_ref, shc_ref, scb_ref, scc_ref, h_scr)
    is_q = pl.program_id(1) < QK_WIDTH // TN_QKV
    post = jnp.where(is_q, Q_SCALE, 1.0)
    nw = jnp.where(is_q, qn_ref[0:1, :], kn_ref[0:1, :])
    nw_rot = jnp.where(is_q, qn_ref[1:2, :], kn_ref[1:2, :])
    for half in range(2):
        cols = slice(half * TN_HALF, (half + 1) * TN_HALF)
        acc = jnp.dot(h_scr[...], w_ref[:, cols].astype(BF16), preferred_element_type=F32)
        acc_scr[half] = acc
        for blk in range(TN_HALF // (2 * HEAD_DIM)):
            bc = slice(blk * 2 * HEAD_DIM, (blk + 1) * 2 * HEAD_DIM)
            a = acc[:, bc]
            rot_scr[half, :, bc] = jnp.dot(a.astype(BF16), swap_ref[...], preferred_element_type=F32)
            ssq_scr[half, :, bc] = jnp.dot((a * a).astype(BF16), ones_ref[...], preferred_element_type=F32)
        for r0 in range(0, TM_QKV, EP_ROWS):
            rows = slice(r0, r0 + EP_ROWS)
            w_cos = nw * cos_ref[rows, :]
            w_sin = nw_rot * sin_ref[rows, :]
            outs = []
            for c in range(TN_HALF // HEAD_DIM):
                cc = slice(c * HEAD_DIM, (c + 1) * HEAD_DIM)
                scale = lax.rsqrt(ssq_scr[half, rows, cc] * (1.0 / HEAD_DIM) + NORM_EPS) * post
                outs.append(((acc_scr[half, rows, cc] * w_cos + rot_scr[half, rows, cc] * w_sin)
                             * scale).astype(BF16))
            o_ref[rows, half * TN_HALF:(half + 1) * TN_HALF] = jnp.concatenate(outs, axis=1)


def _v_kernel(x_ref, nw_ref, shb_ref, shc_ref, scb_ref, scc_ref, w_ref, o_ref, h_scr):
    _qkv_prologue(x_ref, nw_ref, shb_ref, shc_ref, scb_ref, scc_ref, h_scr)
    for half in range(2):
        cols = slice(half * TN_HALF, (half + 1) * TN_HALF)
        o_ref[:, cols] = jnp.dot(h_scr[...], w_ref[:, cols].astype(BF16),
                                 preferred_element_type=F32).astype(BF16)


def _qkv(xs, mods, norm_w, w_qkv, layer, q_norm, k_norm, cos, sin):
    shb, shc = _mod_specs(0, TM_QKV)
    scb, scc = _mod_specs(1, TM_QKV)
    tiles_per_batch = NZ // TM_QKV
    n_qk = 2 * QK_WIDTH // TN_QKV
    n_v = HEADS * V_DIM // TN_QKV
    row_specs = [pl.BlockSpec((TM_QKV, D), lambda i, j: (i, 0)),
                 pl.BlockSpec((1, D), lambda i, j: (0, 0)), shb, shc, scb, scc]
    row_args = (xs, norm_w.reshape(1, D), mods, mods, mods, mods)
    head_vec = pl.BlockSpec((2, HEAD_DIM), lambda i, j: (0, 0))
    lane_mat = pl.BlockSpec((2 * HEAD_DIM, 2 * HEAD_DIM), lambda i, j: (0, 0))
    swap, ones = _head_lane_matrices()
    ep_scratch = pltpu.VMEM((2, TM_QKV, TN_HALF), F32)
    rope = pl.BlockSpec((TM_QKV, HEAD_DIM), lambda i, j: (i % tiles_per_batch, 0))
    h_scratch = pltpu.VMEM((TM_QKV, D), BF16)
    common = dict(out_specs=pl.BlockSpec((TM_QKV, TN_QKV), lambda i, j: (i, j)),
                  compiler_params=_cparams(("arbitrary", "arbitrary")))
    qk = pl.pallas_call(
        _qk_kernel,
        grid=(T // TM_QKV, n_qk),
        in_specs=row_specs + [pl.BlockSpec((None, D, TN_QKV), lambda i, j: (layer, 0, j)),
                              head_vec, head_vec, rope, rope, lane_mat, lane_mat],
        out_shape=jax.ShapeDtypeStruct((T, 2 * QK_WIDTH), BF16),
        scratch_shapes=[h_scratch, ep_scratch, ep_scratch, ep_scratch], **common,
    )(*row_args, w_qkv, jnp.stack([q_norm, q_norm[_ROT_PARTNER]]), jnp.stack([k_norm, k_norm[_ROT_PARTNER]]),
      cos, sin, swap, ones)
    v = pl.pallas_call(
        _v_kernel,
        grid=(T // TM_QKV, n_v),
        in_specs=row_specs + [pl.BlockSpec((None, D, TN_QKV), lambda i, j: (layer, 0, j + n_qk))],
        out_shape=jax.ShapeDtypeStruct((T, HEADS * V_DIM), BF16), scratch_shapes=[h_scratch], **common,
    )(*row_args, w_qkv)
    return qk, v


TQ = 256
Q_TILES = NZ // TQ


SM_ROWS = 16


def _attn_kernel(lam_ref, q_ref, qn_ref, k_ref, v_ref, sub_ref, o_ref,
                 s_even, s_odd, p_even, p_odd, inv_even, inv_odd, o_scr, *, lam_init):
    lv = lam_ref[...]
    lam = (jnp.exp(jnp.sum(lv[0:1] * lv[1:2], axis=-1, keepdims=True))
           - jnp.exp(jnp.sum(lv[2:3] * lv[3:4], axis=-1, keepdims=True)) + lam_init)

    def scores(q_blk, s_dst, n_keys):
        q = q_blk[...]
        for m in range(2):
            cols = slice(m * HEAD_DIM, (m + 1) * HEAD_DIM)
            s_dst[m, :, 0:n_keys] = lax.dot_general(q[:, cols], k_ref[0:n_keys, cols], (((1,), (1,)), ((), ())),
                                                    preferred_element_type=F32)

    def softmax(s_src, p_dst, inv_dst, n_keys):
        for m in range(2):
            for r0 in range(0, TQ, SM_ROWS):
                rows = slice(r0, r0 + SM_ROWS)
                s = s_src[m, rows, 0:n_keys]
                p = jnp.exp2(s - jnp.max(s, axis=-1, keepdims=True))
                inv_dst[m, rows, :] = jnp.broadcast_to(1.0 / jnp.sum(p, axis=-1, keepdims=True), (SM_ROWS, LANE))
                p_dst[m, rows, 0:n_keys] = p.astype(BF16)

    def values(p_src, inv_src, n_keys):
        for m in range(2):
            o_scr[m] = jnp.dot(p_src[m, :, 0:n_keys], v_ref[0:n_keys, :], preferred_element_type=F32)
        for r0 in range(0, TQ, SM_ROWS):
            rows = slice(r0, r0 + SM_ROWS)
            inv = [jnp.concatenate([inv_src[m, rows, :]] * (V_DIM // LANE), axis=1) for m in range(2)]
            o = o_scr[0, rows, :] * inv[0] - lam * (o_scr[1, rows, :] * inv[1])
            o = o * lax.rsqrt(jnp.mean(o * o, axis=-1, keepdims=True) + SUBLN_EPS) * sub_ref[...]
            o_ref[rows, :] = (o * (1.0 - lam_init)).astype(BF16)

    t = pl.program_id(2)

    @pl.when(t == 0)
    def _():
        scores(q_ref, s_even, CTX)
        scores(qn_ref, s_odd, NZ)
        softmax(s_even, p_even, inv_even, CTX)

    @pl.when(t == 1)
    def _():
        values(p_even, inv_even, CTX)
        scores(qn_ref, s_even, NZ)
        softmax(s_odd, p_odd, inv_odd, NZ)

    @pl.when((t > 1) & (t < Q_TILES) & (t % 2 == 0))
    def _():
        values(p_odd, inv_odd, NZ)
        scores(qn_ref, s_odd, NZ)
        softmax(s_even, p_even, inv_even, NZ)

    @pl.when((t > 1) & (t < Q_TILES) & (t % 2 == 1))
    def _():
        values(p_even, inv_even, NZ)
        scores(qn_ref, s_even, NZ)
        softmax(s_odd, p_odd, inv_odd, NZ)

    @pl.when(t == Q_TILES)
    def _():
        values(p_even if (Q_TILES - 1) % 2 == 0 else p_odd, inv_even if (Q_TILES - 1) % 2 == 0 else inv_odd, NZ)


def _attention(qk, v, lam_vecs, subln, lam_init):
    last = Q_TILES - 1
    s_buf = pltpu.VMEM((2, TQ, NZ), F32)
    p_buf = pltpu.VMEM((2, TQ, NZ), BF16)
    inv_buf = pltpu.VMEM((2, TQ, LANE), F32)
    return pl.pallas_call(
        functools.partial(_attn_kernel, lam_init=lam_init),
        grid=(B, HEADS, Q_TILES + 1),
        in_specs=[pl.BlockSpec((4, HEAD_DIM), lambda b, h, t: (0, 0)),
                  pl.BlockSpec((TQ, V_DIM), lambda b, h, t: (b * Q_TILES, h)),
                  pl.BlockSpec((TQ, V_DIM), lambda b, h, t: (b * Q_TILES + jnp.minimum(t + 1, last), h)),
                  pl.BlockSpec((NZ, V_DIM), lambda b, h, t: (b, HEADS + h)),
                  pl.BlockSpec((NZ, V_DIM), lambda b, h, t: (b, h)),
                  pl.BlockSpec((1, V_DIM), lambda b, h, t: (0, 0))],
        out_specs=pl.BlockSpec((TQ, V_DIM), lambda b, h, t: (b * Q_TILES + jnp.maximum(t - 1, 0), h)),
        out_shape=jax.ShapeDtypeStruct((T, HEADS * V_DIM), BF16),
        scratch_shapes=[s_buf, s_buf, p_buf, p_buf, inv_buf, inv_buf, pltpu.VMEM((2, TQ, V_DIM), F32)],
        compiler_params=_cparams(("arbitrary", "arbitrary", "arbitrary")),
    )(lam_vecs, qk, qk, qk, v, subln.reshape(1, V_DIM))


def _proj_res_kernel(a_ref, w_ref, x_ref, gb_ref, gc_ref, o_ref):
    acc = jnp.dot(a_ref[...], w_ref[...].astype(BF16), preferred_element_type=F32)
    gate = _pick(_is_ctx_rows(TM_MM), gb_ref, gc_ref)
    o_ref[...] = x_ref[...] + gate * acc


def _proj_res(a, w, layer, xs, mods, gate_chunk):
    gb, gc = _mod_specs(gate_chunk, TM_MM, TN_MM)
    return pl.pallas_call(
        _proj_res_kernel,
        grid=(T // TM_MM, D // TN_MM),
        in_specs=[pl.BlockSpec((TM_MM, a.shape[1]), lambda i, j: (i, 0)),
                  pl.BlockSpec((None, a.shape[1], TN_MM), lambda i, j: (layer, 0, j)),
                  pl.BlockSpec((TM_MM, TN_MM), lambda i, j: (i, j)),
                  gb, gc],
        out_specs=pl.BlockSpec((TM_MM, TN_MM), lambda i, j: (i, j)),
        out_shape=jax.ShapeDtypeStruct((T, D), F32),
        compiler_params=_cparams(("arbitrary", "arbitrary")),
    )(a, w, xs, mods, mods)


def _discretize_kernel(are_ref, aim_ref, ldt_ref, bre_ref, bim_ref, abr_ref, abi_ref, bbr_ref, bbi_ref):
    a_re = jnp.minimum(are_ref[...], -1e-4)
    a_im = aim_ref[...]
    dt = jnp.exp(ldt_ref[...])
    mag = jnp.exp(a_re * dt)
    abar_re = mag * jnp.cos(a_im * dt)
    abar_im = mag * jnp.sin(a_im * dt)
    den = a_re * a_re + a_im * a_im
    f_re = ((abar_re - 1.0) * a_re + abar_im * a_im) / den
    f_im = (abar_im * a_re - (abar_re - 1.0) * a_im) / den
    b_re = bre_ref[...]
    b_im = bim_ref[...]
    abr_ref[...] = abar_re
    abi_ref[...] = abar_im
    bbr_ref[...] = f_re * b_re - f_im * b_im
    bbi_ref[...] = f_re * b_im + f_im * b_re


def _ssm_operators(a_re, a_im, log_dt, b_re, b_im, c_re, c_im):
    g, p, ch = SSM_GROUPS, SSM_STATE, SSM_CH
    rows, width = 2 * g, p * ch
    rep = lambda a: jnp.broadcast_to(a[..., None], (2, g, p, ch)).reshape(rows, width)
    spec = pl.BlockSpec((rows, width), lambda: (0, 0))
    abr, abi, bbr, bbi = pl.pallas_call(
        _discretize_kernel,
        in_specs=[spec] * 5,
        out_specs=[spec] * 4,
        out_shape=[jax.ShapeDtypeStruct((rows, width), F32)] * 4,
        compiler_params=pltpu.CompilerParams(vmem_limit_bytes=VMEM_LIMIT),
    )(rep(a_re), rep(a_im), rep(jnp.broadcast_to(log_dt[..., None], (2, g, p))),
      b_re.reshape(rows, width), b_im.reshape(rows, width))
    nj, gl = N_LANE_BLOCKS, GROUPS_PER_LANE_BLOCK

    def a_tiles(a):
        a = a.reshape(2, g, p, ch)[..., 0].reshape(2, nj, STATE_LANES).transpose(1, 0, 2)
        return jnp.repeat(a, B, axis=1)

    eye = jnp.eye(gl, dtype=F32)
    bb = jnp.stack([bbr, bbi]).reshape(2, 2, nj, gl, p, ch)
    w_drive = jnp.einsum('ab,rdjapc->jdacrbp', eye, bb).reshape(nj, 2 * LANE, 2 * STATE_LANES)
    cc = jnp.stack([c_re, -c_im]).reshape(2, 2, nj, gl, ch, p)
    w_read = jnp.einsum('ab,rdjacp->jrapdbc', eye, cc).reshape(nj, 2 * STATE_LANES, 2 * LANE)
    return a_tiles(abr), a_tiles(abi), w_drive.astype(BF16), w_read.astype(BF16)


def _scan_perm():
    perm = np.zeros((SCAN_ROWS, SCAN_ROWS), np.float32)
    for tau in range(SCAN_SUB):
        for s in range(SCAN_SEQS):
            src = tau if s < B else SCAN_SUB - 1 - tau
            perm[tau * SCAN_SEQS + s, s * SCAN_SUB + src] = 1.0
    return jnp.asarray(perm, BF16), jnp.asarray(perm.T, BF16)


def _prenorm_kernel(x_ref, nw_ref, shb_ref, shc_ref, scb_ref, scc_ref, o_ref):
    is_ctx = _is_ctx_rows(TM_EW)
    h = _norm_mod(x_ref[...], nw_ref[...], _pick(is_ctx, shb_ref, shc_ref), _pick(is_ctx, scb_ref, scc_ref))
    o_ref[...] = h.astype(o_ref.dtype)


def _prenorm(xs, mods, norm_w, shift_chunk, dtype):
    shb, shc = _mod_specs(shift_chunk, TM_EW)
    scb, scc = _mod_specs(shift_chunk + 1, TM_EW)
    return pl.pallas_call(
        _prenorm_kernel,
        grid=(T // TM_EW,),
        in_specs=[pl.BlockSpec((TM_EW, D), lambda i: (i, 0)),
                  pl.BlockSpec((1, D), lambda i: (0, 0)), shb, shc, scb, scc],
        out_specs=pl.BlockSpec((TM_EW, D), lambda i: (i, 0)),
        out_shape=jax.ShapeDtypeStruct((T, D), dtype),
        compiler_params=_cparams(("arbitrary",)),
    )(xs, norm_w.reshape(1, D), mods, mods, mods, mods)


SCAN_CHUNK = CTX
N_SCAN_CHUNKS = NZ // SCAN_CHUNK
SCAN_PARTS = 2
SUBS_PER_PART = SCAN_CHUNK // SCAN_SUB // SCAN_PARTS


def _bwd_chunk(ci):
    return jnp.where(ci == 0, 0, N_SCAN_CHUNKS - ci)


def _scan_kernel(hf_ref, hb_ref, are_ref, aim_ref, wd_ref, wr_ref, perm_ref, permt_ref,
                 yf_ref, yb_ref, sre_scr, sim_scr, bu_scr):
    @pl.when(pl.program_id(1) == 0)
    def _():
        sre_scr[...] = jnp.zeros_like(sre_scr)
        sim_scr[...] = jnp.zeros_like(sim_scr)

    a_re = are_ref[0]
    a_im = aim_ref[0]
    row = lax.broadcasted_iota(jnp.int32, (SCAN_ROWS, LANE), 0)
    is_fwd = (row % SCAN_SEQS) < B

    def windows(sub):
        off_f = sub * SCAN_SUB
        off_b = SCAN_CHUNK - SCAN_SUB - sub * SCAN_SUB
        return slice(off_f, off_f + SCAN_SUB), slice(off_b, off_b + SCAN_SUB)

    for part in range(SCAN_PARTS):
        lhs = []
        for q in range(SUBS_PER_PART):
            win_f, win_b = windows(part * SUBS_PER_PART + q)
            win = jnp.concatenate([hf_ref[:, win_f, :].reshape(B * SCAN_SUB, LANE),
                                   hb_ref[:, win_b, :].reshape(B * SCAN_SUB, LANE)], axis=0)
            u = jnp.dot(perm_ref[...], win, preferred_element_type=F32)
            zero = jnp.zeros_like(u)
            lhs.append(jnp.concatenate([jnp.where(is_fwd, u, zero), jnp.where(is_fwd, zero, u)],
                                       axis=1).astype(BF16))
        bu_scr[part] = jnp.dot(jnp.concatenate(lhs, axis=0), wd_ref[0], preferred_element_type=F32)

    s_re = sre_scr[...]
    s_im = sim_scr[...]
    for part in range(SCAN_PARTS):
        bu = bu_scr.at[part]
        for tau in range(SUBS_PER_PART * SCAN_SUB):
            rows = slice(tau * SCAN_SEQS, (tau + 1) * SCAN_SEQS)
            n_re = a_re * s_re - a_im * s_im + bu[rows, 0:STATE_LANES]
            n_im = a_re * s_im + a_im * s_re + bu[rows, STATE_LANES:2 * STATE_LANES]
            s_re, s_im = n_re, n_im
            bu[rows, 0:STATE_LANES] = s_re
            bu[rows, STATE_LANES:2 * STATE_LANES] = s_im
        half = SUBS_PER_PART * SCAN_ROWS // 2
        y2 = [jnp.dot(bu[r0:r0 + half, :].astype(BF16), wr_ref[0], preferred_element_type=F32)
              for r0 in (0, half)]
        for q in range(SUBS_PER_PART):
            win_f, win_b = windows(part * SUBS_PER_PART + q)
            r0 = q * SCAN_ROWS % half
            yq = y2[q * SCAN_ROWS // half][r0:r0 + SCAN_ROWS]
            y = jnp.where(is_fwd, yq[:, 0:LANE], yq[:, LANE:2 * LANE])
            yt = jnp.dot(permt_ref[...], y.astype(BF16),
                         preferred_element_type=F32).astype(BF16)
            yf_ref[:, win_f, :] = yt[0:B * SCAN_SUB].reshape(B, SCAN_SUB, LANE)
            yb_ref[:, win_b, :] = yt[B * SCAN_SUB:].reshape(B, SCAN_SUB, LANE)
    sre_scr[...] = s_re
    sim_scr[...] = s_im


def _ssm_scan(h, a_re_t, a_im_t, w_drive, w_read, perm, perm_t):
    h3 = h.reshape(B, NZ, D)
    blk = (B, SCAN_CHUNK, LANE)
    fwd_spec = pl.BlockSpec(blk, lambda j, ci: (0, ci, j))
    bwd_spec = pl.BlockSpec(blk, lambda j, ci: (0, _bwd_chunk(ci), j))
    a_spec = pl.BlockSpec((1, SCAN_SEQS, STATE_LANES), lambda j, ci: (j, 0, 0))
    p_spec = pl.BlockSpec((SCAN_ROWS, SCAN_ROWS), lambda j, ci: (0, 0))
    yf, yb = pl.pallas_call(
        _scan_kernel,
        grid=(N_LANE_BLOCKS, N_SCAN_CHUNKS),
        in_specs=[fwd_spec, bwd_spec, a_spec, a_spec,
                  pl.BlockSpec((1, 2 * LANE, 2 * STATE_LANES), lambda j, ci: (j, 0, 0)),
                  pl.BlockSpec((1, 2 * STATE_LANES, 2 * LANE), lambda j, ci: (j, 0, 0)),
                  p_spec, p_spec],
        out_specs=[fwd_spec, bwd_spec],
        out_shape=[jax.ShapeDtypeStruct((B, NZ, D), BF16)] * 2,
        scratch_shapes=[pltpu.VMEM((SCAN_SEQS, STATE_LANES), F32),
                        pltpu.VMEM((SCAN_SEQS, STATE_LANES), F32),
                        pltpu.VMEM((SCAN_PARTS, SUBS_PER_PART * SCAN_ROWS, 2 * STATE_LANES), F32)],
        compiler_params=_cparams(("arbitrary", "arbitrary")),
    )(h3, h3, a_re_t, a_im_t, w_drive, w_read, perm, perm_t)
    return yf.reshape(T, D), yb.reshape(T, D)


def _ssm_post_kernel(x_ref, nw_ref, shb_ref, shc_ref, scb_ref, scc_ref, d_ref, yf_ref, yb_ref, o_ref):
    is_ctx = _is_ctx_rows(TM_EW)
    h = _norm_mod(x_ref[...], nw_ref[...], _pick(is_ctx, shb_ref, shc_ref), _pick(is_ctx, scb_ref, scc_ref))
    y = d_ref[...] * h + yf_ref[...] + yb_ref[...]
    o_ref[...] = jax.nn.gelu(y).astype(BF16)


def _ssm_post(xs, mods, norm_w, d_skip, yf, yb):
    shb, shc = _mod_specs(0, TM_EW)
    scb, scc = _mod_specs(1, TM_EW)
    row = pl.BlockSpec((TM_EW, D), lambda i: (i, 0))
    vec = pl.BlockSpec((1, D), lambda i: (0, 0))
    return pl.pallas_call(
        _ssm_post_kernel,
        grid=(T // TM_EW,),
        in_specs=[row, vec, shb, shc, scb, scc, vec, row, row],
        out_specs=row,
        out_shape=jax.ShapeDtypeStruct((T, D), BF16),
        compiler_params=_cparams(("arbitrary",)),
    )(xs, norm_w.reshape(1, D), mods, mods, mods, mods, d_skip.reshape(1, D), yf, yb)


def _glu_res_kernel(a_ref, wa_ref, wb_ref, ba_ref, bb_ref, x_ref, gb_ref, gc_ref, o_ref):
    a = a_ref[...]
    za = jnp.dot(a, wa_ref[...].astype(BF16), preferred_element_type=F32) + ba_ref[...]
    zb = jnp.dot(a, wb_ref[...].astype(BF16), preferred_element_type=F32) + bb_ref[...]
    gate = _pick(_is_ctx_rows(TM_MM), gb_ref, gc_ref)
    o_ref[...] = x_ref[...] + gate * (za * jax.nn.sigmoid(zb))


def _glu_res(a, w_glu, layer, b_glu, xs, mods):
    gb, gc = _mod_specs(2, TM_MM, TN_MM)
    nb = D // TN_MM
    return pl.pallas_call(
        _glu_res_kernel,
        grid=(T // TM_MM, nb),
        in_specs=[pl.BlockSpec((TM_MM, D), lambda i, j: (i, 0)),
                  pl.BlockSpec((None, D, TN_MM), lambda i, j: (layer, 0, j)),
                  pl.BlockSpec((None, D, TN_MM), lambda i, j: (layer, 0, j + nb)),
                  pl.BlockSpec((1, TN_MM), lambda i, j: (0, j)),
                  pl.BlockSpec((1, TN_MM), lambda i, j: (0, j + nb)),
                  pl.BlockSpec((TM_MM, TN_MM), lambda i, j: (i, j)),
                  gb, gc],
        out_specs=pl.BlockSpec((TM_MM, TN_MM), lambda i, j: (i, j)),
        out_shape=jax.ShapeDtypeStruct((T, D), F32),
        compiler_params=_cparams(("arbitrary", "arbitrary")),
    )(a, w_glu, w_glu, b_glu.reshape(1, 2 * D), b_glu.reshape(1, 2 * D), xs, mods, mods)


def _tok_rows_load(ref, n):
    return jnp.concatenate([ref[pl.ds(j, n, stride=ROW_PITCH), :] for j in range(ROW_TILE)], axis=1)


def _tok_rows_store(ref, val, n):
    for j in range(ROW_TILE):
        ref[pl.ds(j, n, stride=ROW_TILE), :] = val[:, j * LANE:(j + 1) * LANE]


def _route_kernel(x_ref, nw_ref, shb_ref, shc_ref, scb_ref, scc_ref, wr_ref, br_ref,
                  h_ref, ri_ref, rw_ref, cnt_ref, carry_scr):
    @pl.when(pl.program_id(0) == 0)
    def _():
        carry_scr[...] = jnp.zeros_like(carry_scr)

    is_ctx = _is_ctx_rows(TM_RT)
    h = _norm_mod(x_ref[...], nw_ref[...], _pick(is_ctx, shb_ref, shc_ref), _pick(is_ctx, scb_ref, scc_ref))
    _tok_rows_store(h_ref, h, TM_RT)

    w = wr_ref[...]
    h_hi = h.astype(BF16)
    h_lo = (h - h_hi.astype(F32)).astype(BF16)
    w_hi = w.astype(BF16)
    w_lo = (w - w_hi.astype(F32)).astype(BF16)
    logits = (jnp.dot(h_hi, w_hi, preferred_element_type=F32)
              + jnp.dot(h_hi, w_lo, preferred_element_type=F32)
              + jnp.dot(h_lo, w_hi, preferred_element_type=F32)) + br_ref[...]

    lane = lax.broadcasted_iota(jnp.int32, (TM_RT, ROUTE_LANES), 1).astype(F32)
    big = float(ROUTE_LANES)
    neg = -jnp.inf
    is_g = lane < N_GROUPS
    g_max = jnp.max(jnp.where(is_g, logits, neg), axis=-1, keepdims=True)
    g_sum = jnp.sum(jnp.where(is_g, jnp.exp(logits - g_max), 0.0), axis=-1, keepdims=True)
    g_p = 1.0 / g_sum
    g_idx = jnp.min(jnp.where(is_g, jnp.where(logits == g_max, lane, big), big), axis=-1, keepdims=True)
    lo = N_GROUPS + N_EPG * g_idx
    e_log = jnp.where(lane >= lo, jnp.where(lane < lo + N_EPG, logits, neg), neg)
    e1 = jnp.max(e_log, axis=-1, keepdims=True)
    i1 = jnp.min(jnp.where(e_log == e1, lane, big), axis=-1, keepdims=True)
    e_log2 = jnp.where(lane == i1, neg, e_log)
    e2 = jnp.max(e_log2, axis=-1, keepdims=True)
    i2 = jnp.min(jnp.where(e_log2 == e2, lane, big), axis=-1, keepdims=True)
    p2 = jnp.exp(e2 - e1)
    w1 = g_p / (1.0 + p2)
    w2 = g_p * p2 / (1.0 + p2)
    x1 = i1 - N_GROUPS
    x2 = i2 - N_GROUPS

    sel1 = lane == x1
    sel2 = lane == x2
    onehot = jnp.where(sel1, 1.0, jnp.where(sel2, 1.0, 0.0))
    r_i = lax.broadcasted_iota(jnp.int32, (TM_RT, TM_RT), 0)
    c_i = lax.broadcasted_iota(jnp.int32, (TM_RT, TM_RT), 1)
    tril = jnp.where(r_i > c_i, 1.0, 0.0).astype(BF16)
    before = jnp.dot(tril, onehot.astype(BF16), preferred_element_type=F32) + carry_scr[0:1, :]
    rank1 = jnp.sum(jnp.where(sel1, before, 0.0), axis=-1, keepdims=True)
    rank2 = jnp.sum(jnp.where(sel2, before, 0.0), axis=-1, keepdims=True)
    total = carry_scr[0:1, :] + jnp.sum(onehot, axis=0, keepdims=True)
    carry_scr[...] = jnp.broadcast_to(total, carry_scr.shape)
    cnt_ref[...] = jnp.broadcast_to(total, cnt_ref.shape)

    ri = jnp.where(lane == 0, x1, jnp.where(lane == 1, x2, jnp.where(lane == 2, rank1, jnp.where(lane == 3, rank2, 0.0))))
    ri_ref[...] = ri.astype(jnp.int32)
    rw_ref[...] = jnp.where(lane == 0, w1, jnp.where(lane == 1, w2, 0.0))


def _route(xs, mods, norm_w, w_rg, b_rg, w_re, b_re):
    pad = ROUTE_LANES - N_GROUPS - N_EXPERTS
    w_cat = jnp.concatenate([w_rg, w_re.reshape(D, N_EXPERTS), jnp.zeros((D, pad), F32)], axis=1)
    b_cat = jnp.concatenate([b_rg, b_re.reshape(N_EXPERTS), jnp.zeros((pad,), F32)]).reshape(1, ROUTE_LANES)
    shb, shc = _mod_specs(3, TM_RT)
    scb, scc = _mod_specs(4, TM_RT)
    lanes = pl.BlockSpec((TM_RT, ROUTE_LANES), lambda i: (i, 0))
    return pl.pallas_call(
        _route_kernel,
        grid=(T // TM_RT,),
        in_specs=[pl.BlockSpec((TM_RT, D), lambda i: (i, 0)),
                  pl.BlockSpec((1, D), lambda i: (0, 0)), shb, shc, scb, scc,
                  pl.BlockSpec((D, ROUTE_LANES), lambda i: (0, 0)),
                  pl.BlockSpec((1, ROUTE_LANES), lambda i: (0, 0))],
        out_specs=[pl.BlockSpec((TM_RT * ROW_TILE, LANE), lambda i: (i, 0)), lanes, lanes,
                   pl.BlockSpec((SUBLANE, ROUTE_LANES), lambda i: (0, 0))],
        out_shape=[jax.ShapeDtypeStruct((T * ROW_TILE, LANE), F32),
                   jax.ShapeDtypeStruct((T, ROUTE_LANES), jnp.int32),
                   jax.ShapeDtypeStruct((T, ROUTE_LANES), F32),
                   jax.ShapeDtypeStruct((SUBLANE, ROUTE_LANES), F32)],
        scratch_shapes=[pltpu.VMEM((SUBLANE, ROUTE_LANES), F32)],
        compiler_params=_cparams(("arbitrary",)),
    )(xs, norm_w.reshape(1, D), mods, mods, mods, mods, w_cat, b_cat)


def _row_gather(src_hbm, off_ref, off_index, dst, sem, n, wait):
    def body(g, c):
        for i in range(GATHER_UNROLL):
            r = g * GATHER_UNROLL + i
            off = 0 if wait else pl.multiple_of(off_ref[off_index(r)], ROW_TILE)
            cp = pltpu.make_async_copy(src_hbm.at[pl.ds(off, ROW_TILE), :],
                                       dst.at[pl.ds(pl.multiple_of(r * ROW_PITCH, SUBLANE), ROW_TILE), :], sem)
            if wait:
                cp.wait()
            else:
                cp.start(priority=i % 2)
        return c

    lax.fori_loop(0, n // GATHER_UNROLL, body, 0)


def _expert_kernel(be_ref, nv_ref, ro_ref, h_hbm, w1_ref, w3_ref, w2_ref, y_ref, x_scr, sems):
    blk = pl.program_id(0)
    n_valid = nv_ref[0]
    slot = blk % 2

    def gather(block, into, wait):
        _row_gather(h_hbm, ro_ref, lambda r: block * TE + r, x_scr.at[into], sems.at[into], TE, wait)

    @pl.when(blk == 0)
    def _():
        gather(0, 0, False)

    @pl.when(blk + 1 < n_valid)
    def _():
        gather(blk + 1, 1 - slot, False)

    @pl.when(blk < n_valid)
    def _():
        gather(blk, slot, True)
        x = _tok_rows_load(x_scr.at[slot], TE).astype(BF16)
        a = jnp.dot(x, w1_ref[0].astype(BF16), preferred_element_type=F32)
        c = jnp.dot(x, w3_ref[0].astype(BF16), preferred_element_type=F32)
        mid = (jax.nn.silu(a) * c).astype(BF16)
        _tok_rows_store(y_ref, jnp.dot(mid, w2_ref[0].astype(BF16), preferred_element_type=F32), TE)

    @pl.when(blk >= n_valid)
    def _():
        y_ref[...] = jnp.zeros_like(y_ref)


def _experts(h_rows, block_expert, n_valid, row_off, w1, w3, w2):
    grid_spec = pltpu.PrefetchScalarGridSpec(
        num_scalar_prefetch=3,
        grid=(N_EBLOCKS,),
        in_specs=[pl.BlockSpec(memory_space=pl.ANY),
                  pl.BlockSpec((1, D, MOE_F), lambda b, be, nv, ro: (be[b], 0, 0)),
                  pl.BlockSpec((1, D, MOE_F), lambda b, be, nv, ro: (be[b], 0, 0)),
                  pl.BlockSpec((1, MOE_F, D), lambda b, be, nv, ro: (be[b], 0, 0))],
        out_specs=pl.BlockSpec((TE * ROW_TILE, LANE), lambda b, be, nv, ro: (jnp.minimum(b, nv[0]), 0)),
        scratch_shapes=[pltpu.VMEM((2, TE * ROW_PITCH, LANE), F32), pltpu.SemaphoreType.DMA((2,))],
    )
    return pl.pallas_call(
        _expert_kernel,
        grid_spec=grid_spec,
        out_shape=jax.ShapeDtypeStruct((N_EROWS * ROW_TILE, LANE), F32),
        compiler_params=_cparams(("arbitrary",)),
    )(block_expert, n_valid, row_off, h_rows,
      w1.reshape(DEPTH * N_EXPERTS, D, MOE_F), w3.reshape(DEPTH * N_EXPERTS, D, MOE_F),
      w2.reshape(DEPTH * N_EXPERTS, MOE_F, D))


def _combine_kernel(do_ref, y_hbm, rw_ref, x_ref, gb_ref, gc_ref, o_ref, buf, sems):
    tile = pl.program_id(0)
    slot = tile % 2

    def gather(t, into, wait):
        for k in range(MOE_TOPK):
            _row_gather(y_hbm, do_ref, lambda r: (t * TM_RT + r) * MOE_TOPK + k, buf.at[into, k],
                        sems.at[into], TM_RT, wait)

    @pl.when(tile == 0)
    def _():
        gather(0, 0, False)

    @pl.when(tile + 1 < pl.num_programs(0))
    def _():
        gather(tile + 1, 1 - slot, False)

    gather(tile, slot, True)
    rw = rw_ref[...]
    y = (rw[:, 0:1] * _tok_rows_load(buf.at[slot, 0], TM_RT)
         + rw[:, 1:2] * _tok_rows_load(buf.at[slot, 1], TM_RT))
    gate = _pick(_is_ctx_rows(TM_RT), gb_ref, gc_ref)
    o_ref[...] = x_ref[...] + gate * y


def _combine(ys_rows, dest_off, rw, xs, mods):
    gb, gc = _mod_specs(5, TM_RT)
    grid_spec = pltpu.PrefetchScalarGridSpec(
        num_scalar_prefetch=1,
        grid=(T // TM_RT,),
        in_specs=[pl.BlockSpec(memory_space=pl.ANY),
                  pl.BlockSpec((TM_RT, ROUTE_LANES), lambda i, d: (i, 0)),
                  pl.BlockSpec((TM_RT, D), lambda i, d: (i, 0)),
                  gb, gc],
        out_specs=pl.BlockSpec((TM_RT, D), lambda i, d: (i, 0)),
        scratch_shapes=[pltpu.VMEM((2, MOE_TOPK, TM_RT * ROW_PITCH, LANE), F32),
                        pltpu.SemaphoreType.DMA((2,))],
    )
    return pl.pallas_call(
        _combine_kernel,
        grid_spec=grid_spec,
        out_shape=jax.ShapeDtypeStruct((T, D), F32),
        compiler_params=_cparams(("arbitrary",)),
    )(dest_off, ys_rows, rw, xs, mods, mods)


def _moe(xs, mods, layer, norm_w, w_rg, b_rg, w_re, b_re, w1, w3, w2):
    h_rows, ri, rw, cnt = _route(xs, mods, norm_w, w_rg, b_rg, w_re, b_re)
    counts = cnt[0, :N_EXPERTS].astype(jnp.int32)
    padded = (counts + TE - 1) // TE * TE
    pad_end = jnp.cumsum(padded)
    pad_start = pad_end - padded
    dest = (pad_start[ri[:, 0:MOE_TOPK]] + ri[:, MOE_TOPK:2 * MOE_TOPK]).reshape(-1)
    row_off = jnp.zeros((N_EROWS,), jnp.int32).at[dest].set(
        jnp.repeat(jnp.arange(T, dtype=jnp.int32) * ROW_TILE, MOE_TOPK))
    n_valid = pad_end[-1] // TE
    first_row = jnp.minimum(jnp.arange(N_EBLOCKS, dtype=jnp.int32), n_valid - 1) * TE
    block_expert = jnp.sum(pad_end[None, :] <= first_row[:, None], axis=1).astype(jnp.int32)
    block_expert = jnp.minimum(block_expert, N_EXPERTS - 1) + layer * N_EXPERTS
    ys_rows = _experts(h_rows, block_expert, n_valid.reshape(1).astype(jnp.int32), row_off, w1, w3, w2)
    return _combine(ys_rows, (dest * ROW_TILE).astype(jnp.int32), rw, xs, mods)


def kernel(x, c, ctx, c_ctx, ada_w, ada_b, norm1_w, norm2_w, attn_w_qkv, attn_q_norm, attn_k_norm, attn_lam_q1, attn_lam_k1, attn_lam_q2, attn_lam_k2, attn_subln, attn_w_o, ssm_a_re, ssm_a_im, ssm_log_dt, ssm_b_re, ssm_b_im, ssm_c_re, ssm_c_im, ssm_d, ssm_w_glu, ssm_b_glu, moe_w_rg, moe_b_rg, moe_w_re, moe_b_re, moe_w1, moe_w3, moe_w2):
    xs = jnp.concatenate([ctx, x], axis=1).reshape(T, D)
    mods_all = _ada_table(c, c_ctx, ada_w, ada_b)
    cos, sin = _rope_tables()
    perm, perm_t = _scan_perm()
    for i in range(DEPTH):
        j = i // 2
        mods = mods_all[i].reshape(MOD_ROWS * ADA_CHUNKS, 1, D)
        if i % 2 == 0:
            lam_init = 0.8 - 0.6 * math.exp(-0.3 * i)
            qk, v = _qkv(xs, mods, norm1_w[i], attn_w_qkv, j, attn_q_norm[j], attn_k_norm[j], cos, sin)
            lam_vecs = jnp.stack([attn_lam_q1[j], attn_lam_k1[j], attn_lam_q2[j], attn_lam_k2[j]])
            o = _attention(qk, v, lam_vecs, attn_subln[j], lam_init)
            xs = _proj_res(o, attn_w_o, j, xs, mods, 2)
        else:
            ops = _ssm_operators(ssm_a_re[j], ssm_a_im[j], ssm_log_dt[j], ssm_b_re[j], ssm_b_im[j],
                                 ssm_c_re[j], ssm_c_im[j])
            h = _prenorm(xs, mods, norm1_w[i], 0, BF16)
            yf, yb = _ssm_scan(h, *ops, perm, perm_t)
            g = _ssm_post(xs, mods, norm1_w[i], ssm_d[j], yf, yb)
            xs = _glu_res(g, ssm_w_glu, j, ssm_b_glu[j], xs, mods)
        xs = _moe(xs, mods, i, norm2_w[i], moe_w_rg[i], moe_b_rg[i], moe_w_re[i], moe_b_re[i],
                  moe_w1, moe_w3, moe_w2)
    return xs.reshape(B, NZ, D)[:, CTX:, :]
```

```python
import functools
import math

import jax
import jax.numpy as jnp
import numpy as np
from jax import lax
from jax.experimental import pallas as pl
from jax.experimental.pallas import tpu as pltpu

F32 = jnp.float32
BF16 = jnp.bfloat16

D = 2048
B = 4
SEQ = 2048
CTX = 256
NZ = CTX + SEQ
T = B * NZ
DEPTH = 4
GRID_W = 64
NORM_EPS = 1e-6
ADA_CHUNKS = 6
CTX_MOD_ROW = B
MOD_ROWS = 8

HEADS = 8
HEAD_DIM = 128
V_DIM = 2 * HEAD_DIM
QK_WIDTH = HEADS * 2 * HEAD_DIM
DA_SCALE = HEAD_DIM ** -0.5
Q_SCALE = DA_SCALE * math.log2(math.e)
SUBLN_EPS = 1e-5
ROPE_BASE = 10000.0

SSM_CH = 16
SSM_GROUPS = D // SSM_CH
SSM_STATE = 64
LANE = 128
SUBLANE = 8
GROUPS_PER_LANE_BLOCK = LANE // SSM_CH
N_LANE_BLOCKS = D // LANE
STATE_LANES = GROUPS_PER_LANE_BLOCK * SSM_STATE
SCAN_SEQS = 2 * B
SCAN_SUB = 32
SCAN_ROWS = SCAN_SUB * SCAN_SEQS

N_GROUPS = 4
N_EPG = 8
N_EXPERTS = N_GROUPS * N_EPG
MOE_F = 512
MOE_TOPK = 2
ROUTE_LANES = 128
TE = 256
N_EBLOCKS = (T * MOE_TOPK) // TE + N_EXPERTS
N_EROWS = N_EBLOCKS * TE
ROW_TILE = D // LANE
ROW_PITCH = 24
GATHER_UNROLL = 8

TM_MM = 1152
TN_MM = 512
NORM_ROWS = 32
TM_EW = 576
TM_RT = 256
VMEM_LIMIT = 56 * 1024 * 1024


def _cparams(sem):
    return pltpu.CompilerParams(dimension_semantics=sem, vmem_limit_bytes=VMEM_LIMIT)


def _mod_specs(chunk, tm, tn=None, lat_only=False):
    batch = (lambda i: i // (SEQ // tm)) if lat_only else (lambda i: (i * tm) // NZ)
    if tn is None:
        return (pl.BlockSpec((1, 1, D), lambda i, *_: (batch(i) * ADA_CHUNKS + chunk, 0, 0)),
                pl.BlockSpec((1, 1, D), lambda i, *_: (CTX_MOD_ROW * ADA_CHUNKS + chunk, 0, 0)))
    return (pl.BlockSpec((1, 1, tn), lambda i, j: (batch(i) * ADA_CHUNKS + chunk, 0, j)),
            pl.BlockSpec((1, 1, tn), lambda i, j: (CTX_MOD_ROW * ADA_CHUNKS + chunk, 0, j)))


def _is_ctx_rows(tm, lat_only=False):
    if lat_only:
        return False
    z0 = (pl.program_id(0) * tm) % NZ
    return (z0 + lax.broadcasted_iota(jnp.int32, (tm, 1), 0)) < CTX


def _stream_tile(i, tm, lat_only):
    if not lat_only:
        return i
    per_batch = SEQ // tm
    return (i // per_batch) * (NZ // tm) + CTX // tm + i % per_batch


def _pick(is_ctx, b_ref, c_ref):
    return jnp.where(is_ctx, c_ref[0], b_ref[0])


def _norm_mod(x, nw, sh, sc):
    y = x * lax.rsqrt(jnp.mean(x * x, axis=-1, keepdims=True) + NORM_EPS) * nw
    return y * (1.0 + sc) + sh


ADA_TN = 512
ADA_ROWS = B + 1


def _ada_kernel(ct_ref, w_ref, b_ref, o_ref, s_scr):
    @pl.when((pl.program_id(0) == 0) & (pl.program_id(1) == 0))
    def _():
        c = ct_ref[...]
        s = jax.nn.silu(c)
        for r in range(ADA_ROWS):
            s_scr[r] = jnp.broadcast_to(s[:, r:r + 1], (D, LANE))

    nj = ADA_TN // LANE

    def body(kb, accs):
        k0 = pl.multiple_of(kb * SUBLANE, SUBLANE)
        wk = w_ref[0, pl.ds(k0, SUBLANE), :]
        new = []
        for r in range(ADA_ROWS):
            sk = s_scr[r, pl.ds(k0, SUBLANE), :]
            for j in range(nj):
                new.append(accs[r * nj + j] + wk[:, j * LANE:(j + 1) * LANE] * sk)
        return tuple(new)

    zero = jnp.zeros((SUBLANE, LANE), F32)
    accs = lax.fori_loop(0, D // SUBLANE, body, (zero,) * (ADA_ROWS * nj), unroll=4)
    rows = []
    for r in range(ADA_ROWS):
        rows.append(jnp.concatenate(
            [jnp.sum(accs[r * nj + j], axis=0, keepdims=True) for j in range(nj)], axis=1))
    rows.append(jnp.zeros((MOD_ROWS - ADA_ROWS, ADA_TN), F32))
    o_ref[0] = jnp.concatenate(rows, axis=0) + b_ref[0]


def _ada_table(c, c_ctx, ada_w, ada_b):
    cs = jnp.concatenate([c, c_ctx[None, :], jnp.zeros((LANE - ADA_ROWS, D), F32)], axis=0)
    ct = cs.T
    n_out = ADA_CHUNKS * D
    return pl.pallas_call(
        _ada_kernel,
        grid=(DEPTH, n_out // ADA_TN),
        in_specs=[pl.BlockSpec((D, LANE), lambda l, j: (0, 0)),
                  pl.BlockSpec((1, D, ADA_TN), lambda l, j: (l, 0, j)),
                  pl.BlockSpec((1, 1, ADA_TN), lambda l, j: (l, 0, j))],
        out_specs=pl.BlockSpec((1, MOD_ROWS, ADA_TN), lambda l, j: (l, 0, j)),
        out_shape=jax.ShapeDtypeStruct((DEPTH, MOD_ROWS, n_out), F32),
        scratch_shapes=[pltpu.VMEM((ADA_ROWS, D, LANE), F32)],
        compiler_params=_cparams(("arbitrary", "arbitrary")),
    )(ct, ada_w, ada_b.reshape(DEPTH, 1, n_out))


def _rope_tables():
    half = HEAD_DIM // 4
    inv_freq = ROPE_BASE ** (-np.arange(half, dtype=np.float32) / half)
    t = np.arange(SEQ)
    row = (t // GRID_W).astype(np.float32)[:, None] * inv_freq[None, :]
    col = (t % GRID_W).astype(np.float32)[:, None] * inv_freq[None, :]
    cos_l = np.concatenate([np.cos(row), np.cos(row), np.cos(col), np.cos(col)], axis=1)
    sin_l = np.concatenate([-np.sin(row), np.sin(row), -np.sin(col), np.sin(col)], axis=1)
    cos = np.concatenate([np.ones((CTX, HEAD_DIM), np.float32), cos_l.astype(np.float32)], axis=0)
    sin = np.concatenate([np.zeros((CTX, HEAD_DIM), np.float32), sin_l.astype(np.float32)], axis=0)
    return jnp.asarray(cos), jnp.asarray(sin)


TM_QKV = 768
TN_QKV = 1024
TN_HALF = TN_QKV // 2
EP_ROWS = 64


_ROT_PARTNER = np.arange(HEAD_DIM) ^ (HEAD_DIM // 4)


def _head_lane_matrices():
    lanes = np.arange(2 * HEAD_DIM)
    partner = (lanes // HEAD_DIM) * HEAD_DIM + _ROT_PARTNER[lanes % HEAD_DIM]
    swap = (lanes[:, None] == partner[None, :]).astype(np.float32)
    ones = (lanes[:, None] // HEAD_DIM == lanes[None, :] // HEAD_DIM).astype(np.float32)
    return jnp.asarray(swap, BF16), jnp.asarray(ones, BF16)


def _qkv_prologue(x_ref, nw_ref, shb_ref, shc_ref, scb_ref, scc_ref, h_scr):
    @pl.when(pl.program_id(1) == 0)
    def _():
        z0 = (pl.program_id(0) * TM_QKV) % NZ
        for r0 in range(0, TM_QKV, NORM_ROWS):
            rows = slice(r0, r0 + NORM_ROWS)
            is_ctx = z0 + r0 < CTX
            h = _norm_mod(x_ref[rows, :], nw_ref[...], _pick(is_ctx, shb_ref, shc_ref),
                          _pick(is_ctx, scb_ref, scc_ref))
            h_scr[rows, :] = h.astype(BF16)


def _qk_kernel(x_ref, nw_ref, shb_ref, shc_ref, scb_ref, scc_ref, w_ref, qn_ref, kn_ref,
               cos_ref, sin_ref, swap_ref, ones_ref, o_ref, h_scr, acc_scr, rot_scr, ssq_scr):
    _qkv_prologue(x_ref, nw_ref, shb_ref, shc_ref, scb_ref, scc_ref, h_scr)
    is_q = pl.program_id(1) < QK_WIDTH // TN_QKV
    post = jnp.where(is_q, Q_SCALE, 1.0)
    nw = jnp.where(is_q, qn_ref[0:1, :], kn_ref[0:1, :])
    nw_rot = jnp.where(is_q, qn_ref[1:2, :], kn_ref[1:2, :])
    for half in range(2):
        cols = slice(half * TN_HALF, (half + 1) * TN_HALF)
        acc = jnp.dot(h_scr[...], w_ref[:, cols].astype(BF16), preferred_element_type=F32)
        acc_scr[half] = acc
        for blk in range(TN_HALF // (2 * HEAD_DIM)):
            bc = slice(blk * 2 * HEAD_DIM, (blk + 1) * 2 * HEAD_DIM)
            a = acc[:, bc]
            rot_scr[half, :, bc] = jnp.dot(a.astype(BF16), swap_ref[...], preferred_element_type=F32)
            ssq_scr[half, :, bc] = jnp.dot((a * a).astype(BF16), ones_ref[...], preferred_element_type=F32)
        for r0 in range(0, TM_QKV, EP_ROWS):
            rows = slice(r0, r0 + EP_ROWS)
            w_cos = nw * cos_ref[rows, :]
            w_sin = nw_rot * sin_ref[rows, :]
            outs = []
            for c in range(TN_HALF // HEAD_DIM):
                cc = slice(c * HEAD_DIM, (c + 1) * HEAD_DIM)
                scale = lax.rsqrt(ssq_scr[half, rows, cc] * (1.0 / HEAD_DIM) + NORM_EPS) * post
                outs.append(((acc_scr[half, rows, cc] * w_cos + rot_scr[half, rows, cc] * w_sin)
                             * scale).astype(BF16))
            o_ref[rows, half * TN_HALF:(half + 1) * TN_HALF] = jnp.concatenate(outs, axis=1)


def _v_kernel(x_ref, nw_ref, shb_ref, shc_ref, scb_ref, scc_ref, w_ref, o_ref, h_scr):
    _qkv_prologue(x_ref, nw_ref, shb_ref, shc_ref, scb_ref, scc_ref, h_scr)
    for half in range(2):
        cols = slice(half * TN_HALF, (half + 1) * TN_HALF)
        o_ref[:, cols] = jnp.dot(h_scr[...], w_ref[:, cols].astype(BF16),
                                 preferred_element_type=F32).astype(BF16)


def _qkv(xs, mods, norm_w, w_qkv, layer, q_norm, k_norm, cos, sin):
    shb, shc = _mod_specs(0, TM_QKV)
    scb, scc = _mod_specs(1, TM_QKV)
    tiles_per_batch = NZ // TM_QKV
    n_qk = 2 * QK_WIDTH // TN_QKV
    n_v = HEADS * V_DIM // TN_QKV
    row_specs = [pl.BlockSpec((TM_QKV, D), lambda i, j: (i, 0)),
                 pl.BlockSpec((1, D), lambda i, j: (0, 0)), shb, shc, scb, scc]
    row_args = (xs, norm_w.reshape(1, D), mods, mods, mods, mods)
    head_vec = pl.BlockSpec((2, HEAD_DIM), lambda i, j: (0, 0))
    lane_mat = pl.BlockSpec((2 * HEAD_DIM, 2 * HEAD_DIM), lambda i, j: (0, 0))
    swap, ones = _head_lane_matrices()
    ep_scratch = pltpu.VMEM((2, TM_QKV, TN_HALF), F32)
    rope = pl.BlockSpec((TM_QKV, HEAD_DIM), lambda i, j: (i % tiles_per_batch, 0))
    h_scratch = pltpu.VMEM((TM_QKV, D), BF16)
    common = dict(out_specs=pl.BlockSpec((TM_QKV, TN_QKV), lambda i, j: (i, j)),
                  compiler_params=_cparams(("arbitrary", "arbitrary")))
    qk = pl.pallas_call(
        _qk_kernel,
        grid=(T // TM_QKV, n_qk),
        in_specs=row_specs + [pl.BlockSpec((None, D, TN_QKV), lambda i, j: (layer, 0, j)),
                              head_vec, head_vec, rope, rope, lane_mat, lane_mat],
        out_shape=jax.ShapeDtypeStruct((T, 2 * QK_WIDTH), BF16),
        scratch_shapes=[h_scratch, ep_scratch, ep_scratch, ep_scratch], **common,
    )(*row_args, w_qkv, jnp.stack([q_norm, q_norm[_ROT_PARTNER]]), jnp.stack([k_norm, k_norm[_ROT_PARTNER]]),
      cos, sin, swap, ones)
    v = pl.pallas_call(
        _v_kernel,
        grid=(T // TM_QKV, n_v),
        in_specs=row_specs + [pl.BlockSpec((None, D, TN_QKV), lambda i, j: (layer, 0, j + n_qk))],
        out_shape=jax.ShapeDtypeStruct((T, HEADS * V_DIM), BF16), scratch_shapes=[h_scratch], **common,
    )(*row_args, w_qkv)
    return qk, v


TQ = 256
Q_TILES = NZ // TQ


SM_ROWS = 16
HEADS_PER_STEP = 2


def _attn_kernel(lam_ref, q_ref, qn_ref, k_ref, v_ref, sub_ref, o_ref,
                 s_even, s_odd, p_even, p_odd, inv_even, inv_odd, o_scr, *, lam_init):
    lv = lam_ref[...]
    lam = (jnp.exp(jnp.sum(lv[0:1] * lv[1:2], axis=-1, keepdims=True))
           - jnp.exp(jnp.sum(lv[2:3] * lv[3:4], axis=-1, keepdims=True)) + lam_init)

    pairs = [(hh, m) for hh in range(HEADS_PER_STEP) for m in range(2)]

    def scores(q_blk, s_dst, n_keys):
        q = q_blk[...]
        for hh, m in pairs:
            cols = slice((2 * hh + m) * HEAD_DIM, (2 * hh + m + 1) * HEAD_DIM)
            s_dst[hh, m, :, 0:n_keys] = lax.dot_general(q[:, cols], k_ref[0:n_keys, cols], (((1,), (1,)), ((), ())),
                                                        preferred_element_type=F32)

    def softmax(s_src, p_dst, inv_dst, n_keys):
        for hh, m in pairs:
            for r0 in range(0, TQ, SM_ROWS):
                rows = slice(r0, r0 + SM_ROWS)
                s = s_src[hh, m, rows, 0:n_keys]
                p = jnp.exp2(s - jnp.max(s, axis=-1, keepdims=True))
                inv_dst[hh, m, rows, :] = jnp.broadcast_to(1.0 / jnp.sum(p, axis=-1, keepdims=True),
                                                           (SM_ROWS, LANE))
                p_dst[hh, m, rows, 0:n_keys] = p.astype(BF16)

    def values(p_src, inv_src, n_keys):
        for hh, m in pairs:
            o_scr[hh, m] = jnp.dot(p_src[hh, m, :, 0:n_keys], v_ref[0:n_keys, hh * V_DIM:(hh + 1) * V_DIM],
                                   preferred_element_type=F32)
        for hh in range(HEADS_PER_STEP):
            for r0 in range(0, TQ, SM_ROWS):
                rows = slice(r0, r0 + SM_ROWS)
                inv = [jnp.concatenate([inv_src[hh, m, rows, :]] * (V_DIM // LANE), axis=1) for m in range(2)]
                o = o_scr[hh, 0, rows, :] * inv[0] - lam * (o_scr[hh, 1, rows, :] * inv[1])
                o = o * lax.rsqrt(jnp.mean(o * o, axis=-1, keepdims=True) + SUBLN_EPS) * sub_ref[...]
                o_ref[rows, hh * V_DIM:(hh + 1) * V_DIM] = (o * (1.0 - lam_init)).astype(BF16)

    t = pl.program_id(2)

    @pl.when(t == 0)
    def _():
        scores(q_ref, s_even, CTX)
        scores(qn_ref, s_odd, NZ)
        softmax(s_even, p_even, inv_even, CTX)

    @pl.when(t == 1)
    def _():
        values(p_even, inv_even, CTX)
        scores(qn_ref, s_even, NZ)
        softmax(s_odd, p_odd, inv_odd, NZ)

    @pl.when((t > 1) & (t < Q_TILES) & (t % 2 == 0))
    def _():
        values(p_odd, inv_odd, NZ)
        scores(qn_ref, s_odd, NZ)
        softmax(s_even, p_even, inv_even, NZ)

    @pl.when((t > 1) & (t < Q_TILES) & (t % 2 == 1))
    def _():
        values(p_even, inv_even, NZ)
        scores(qn_ref, s_even, NZ)
        softmax(s_odd, p_odd, inv_odd, NZ)

    @pl.when(t == Q_TILES)
    def _():
        values(p_even if (Q_TILES - 1) % 2 == 0 else p_odd, inv_even if (Q_TILES - 1) % 2 == 0 else inv_odd, NZ)


def _attention(qk, v, lam_vecs, subln, lam_init):
    last = Q_TILES - 1
    hps, width = HEADS_PER_STEP, HEADS_PER_STEP * V_DIM
    s_buf = pltpu.VMEM((hps, 2, TQ, NZ), F32)
    p_buf = pltpu.VMEM((hps, 2, TQ, NZ), BF16)
    inv_buf = pltpu.VMEM((hps, 2, TQ, LANE), F32)
    return pl.pallas_call(
        functools.partial(_attn_kernel, lam_init=lam_init),
        grid=(B, HEADS // hps, Q_TILES + 1),
        in_specs=[pl.BlockSpec((4, HEAD_DIM), lambda b, h, t: (0, 0)),
                  pl.BlockSpec((TQ, width), lambda b, h, t: (b * Q_TILES, h)),
                  pl.BlockSpec((TQ, width), lambda b, h, t: (b * Q_TILES + jnp.minimum(t + 1, last), h)),
                  pl.BlockSpec((NZ, width), lambda b, h, t: (b, HEADS // hps + h)),
                  pl.BlockSpec((NZ, width), lambda b, h, t: (b, h)),
                  pl.BlockSpec((1, V_DIM), lambda b, h, t: (0, 0))],
        out_specs=pl.BlockSpec((TQ, width), lambda b, h, t: (b * Q_TILES + jnp.maximum(t - 1, 0), h)),
        out_shape=jax.ShapeDtypeStruct((T, HEADS * V_DIM), BF16),
        scratch_shapes=[s_buf, s_buf, p_buf, p_buf, inv_buf, inv_buf, pltpu.VMEM((hps, 2, TQ, V_DIM), F32)],
        compiler_params=_cparams(("arbitrary", "arbitrary", "arbitrary")),
    )(lam_vecs, qk, qk, qk, v, subln.reshape(1, V_DIM))


def _proj_res_kernel(a_ref, w_ref, x_ref, gb_ref, gc_ref, o_ref):
    acc = jnp.dot(a_ref[...], w_ref[...].astype(BF16), preferred_element_type=F32)
    gate = _pick(_is_ctx_rows(TM_MM), gb_ref, gc_ref)
    o_ref[...] = x_ref[...] + gate * acc


def _proj_res(a, w, layer, xs, mods, gate_chunk):
    gb, gc = _mod_specs(gate_chunk, TM_MM, TN_MM)
    return pl.pallas_call(
        _proj_res_kernel,
        grid=(T // TM_MM, D // TN_MM),
        in_specs=[pl.BlockSpec((TM_MM, a.shape[1]), lambda i, j: (i, 0)),
                  pl.BlockSpec((None, a.shape[1], TN_MM), lambda i, j: (layer, 0, j)),
                  pl.BlockSpec((TM_MM, TN_MM), lambda i, j: (i, j)),
                  gb, gc],
        out_specs=pl.BlockSpec((TM_MM, TN_MM), lambda i, j: (i, j)),
        out_shape=jax.ShapeDtypeStruct((T, D), F32),
        compiler_params=_cparams(("arbitrary", "arbitrary")),
    )(a, w, xs, mods, mods)


def _discretize_kernel(are_ref, aim_ref, ldt_ref, bre_ref, bim_ref, abr_ref, abi_ref, bbr_ref, bbi_ref):
    a_re = jnp.minimum(are_ref[...], -1e-4)
    a_im = aim_ref[...]
    dt = jnp.exp(ldt_ref[...])
    mag = jnp.exp(a_re * dt)
    abar_re = mag * jnp.cos(a_im * dt)
    abar_im = mag * jnp.sin(a_im * dt)
    den = a_re * a_re + a_im * a_im
    f_re = ((abar_re - 1.0) * a_re + abar_im * a_im) / den
    f_im = (abar_im * a_re - (abar_re - 1.0) * a_im) / den
    b_re = bre_ref[...]
    b_im = bim_ref[...]
    abr_ref[...] = abar_re
    abi_ref[...] = abar_im
    bbr_ref[...] = f_re * b_re - f_im * b_im
    bbi_ref[...] = f_re * b_im + f_im * b_re


def _ssm_operators(a_re, a_im, log_dt, b_re, b_im, c_re, c_im):
    g, p, ch = SSM_GROUPS, SSM_STATE, SSM_CH
    rows, width = 2 * g, p * ch
    rep = lambda a: jnp.broadcast_to(a[..., None], (2, g, p, ch)).reshape(rows, width)
    spec = pl.BlockSpec((rows, width), lambda: (0, 0))
    abr, abi, bbr, bbi = pl.pallas_call(
        _discretize_kernel,
        in_specs=[spec] * 5,
        out_specs=[spec] * 4,
        out_shape=[jax.ShapeDtypeStruct((rows, width), F32)] * 4,
        compiler_params=pltpu.CompilerParams(vmem_limit_bytes=VMEM_LIMIT),
    )(rep(a_re), rep(a_im), rep(jnp.broadcast_to(log_dt[..., None], (2, g, p))),
      b_re.reshape(rows, width), b_im.reshape(rows, width))
    nj, gl = N_LANE_BLOCKS, GROUPS_PER_LANE_BLOCK

    def a_tiles(a):
        a = a.reshape(2, g, p, ch)[..., 0].reshape(2, nj, STATE_LANES).transpose(1, 0, 2)
        return jnp.repeat(a, B, axis=1)

    bb = jnp.stack([bbr, bbi]).reshape(2, 2, nj, gl, p, ch)
    drive = bb.transpose(2, 0, 1, 3, 5, 4).reshape(nj, 2, 2 * LANE, p)
    drive = jnp.concatenate([drive, drive], axis=-1)
    cc = jnp.stack([c_re, -c_im]).reshape(2, 2, nj, gl, ch, p)
    read = cc.transpose(2, 0, 5, 1, 3, 4).reshape(nj, 2, p, 2 * LANE)
    return a_tiles(abr), a_tiles(abi), drive, read


def _scan_perm():
    perm = np.zeros((SCAN_ROWS, SCAN_ROWS), np.float32)
    for tau in range(SCAN_SUB):
        for s in range(SCAN_SEQS):
            src = tau if s < B else SCAN_SUB - 1 - tau
            perm[tau * SCAN_SEQS + s, s * SCAN_SUB + src] = 1.0
    return jnp.asarray(perm, BF16), jnp.asarray(perm.T, BF16)


def _prenorm_kernel(x_ref, nw_ref, shb_ref, shc_ref, scb_ref, scc_ref, o_ref):
    is_ctx = _is_ctx_rows(TM_EW)
    h = _norm_mod(x_ref[...], nw_ref[...], _pick(is_ctx, shb_ref, shc_ref), _pick(is_ctx, scb_ref, scc_ref))
    o_ref[...] = h.astype(o_ref.dtype)


def _prenorm(xs, mods, norm_w, shift_chunk, dtype):
    shb, shc = _mod_specs(shift_chunk, TM_EW)
    scb, scc = _mod_specs(shift_chunk + 1, TM_EW)
    return pl.pallas_call(
        _prenorm_kernel,
        grid=(T // TM_EW,),
        in_specs=[pl.BlockSpec((TM_EW, D), lambda i: (i, 0)),
                  pl.BlockSpec((1, D), lambda i: (0, 0)), shb, shc, scb, scc],
        out_specs=pl.BlockSpec((TM_EW, D), lambda i: (i, 0)),
        out_shape=jax.ShapeDtypeStruct((T, D), dtype),
        compiler_params=_cparams(("arbitrary",)),
    )(xs, norm_w.reshape(1, D), mods, mods, mods, mods)


SCAN_CHUNK = CTX
N_SCAN_CHUNKS = NZ // SCAN_CHUNK
SCAN_PARTS = 2
SUBS_PER_PART = SCAN_CHUNK // SCAN_SUB // SCAN_PARTS


def _bwd_chunk(ci):
    return jnp.where(ci == 0, 0, N_SCAN_CHUNKS - ci)


def _scan_kernel(hf_ref, hb_ref, are_ref, aim_ref, drive_ref, read_ref, perm_ref, permt_ref,
                 yf_ref, yb_ref, sre_scr, sim_scr, bu_scr, wd_scr, wr_scr):
    @pl.when(pl.program_id(1) == 0)
    def _():
        sre_scr[...] = jnp.zeros_like(sre_scr)
        sim_scr[...] = jnp.zeros_like(sim_scr)
        gl = GROUPS_PER_LANE_BLOCK
        row_g = (lax.broadcasted_iota(jnp.int32, (2 * LANE, LANE), 0) % LANE) // SSM_CH
        lane_half = lax.broadcasted_iota(jnp.int32, (2 * LANE, LANE), 1) // SSM_STATE
        col_g = (lax.broadcasted_iota(jnp.int32, (SSM_STATE, 2 * LANE), 1) % LANE) // SSM_CH
        for r in range(2):
            for k in range(gl // 2):
                tile = jnp.where(row_g == 2 * k + lane_half, drive_ref[0, r], 0.0)
                v = r * (gl // 2) + k
                wd_scr[:, v * LANE:(v + 1) * LANE] = tile.astype(BF16)
            for g in range(gl):
                rows = slice((r * gl + g) * SSM_STATE, (r * gl + g + 1) * SSM_STATE)
                wr_scr[rows, :] = jnp.where(col_g == g, read_ref[0, r], 0.0).astype(BF16)

    a_re = are_ref[0]
    a_im = aim_ref[0]
    row = lax.broadcasted_iota(jnp.int32, (SCAN_ROWS, LANE), 0)
    is_fwd = (row % SCAN_SEQS) < B

    def windows(sub):
        off_f = sub * SCAN_SUB
        off_b = SCAN_CHUNK - SCAN_SUB - sub * SCAN_SUB
        return slice(off_f, off_f + SCAN_SUB), slice(off_b, off_b + SCAN_SUB)

    for part in range(SCAN_PARTS):
        lhs = []
        for q in range(SUBS_PER_PART):
            win_f, win_b = windows(part * SUBS_PER_PART + q)
            win = jnp.concatenate([hf_ref[:, win_f, :].reshape(B * SCAN_SUB, LANE),
                                   hb_ref[:, win_b, :].reshape(B * SCAN_SUB, LANE)], axis=0)
            u = jnp.dot(perm_ref[...], win, preferred_element_type=F32)
            zero = jnp.zeros_like(u)
            lhs.append(jnp.concatenate([jnp.where(is_fwd, u, zero), jnp.where(is_fwd, zero, u)],
                                       axis=1).astype(BF16))
        bu_scr[part] = jnp.dot(jnp.concatenate(lhs, axis=0), wd_scr[...], preferred_element_type=F32)

    s_re = sre_scr[...]
    s_im = sim_scr[...]
    for part in range(SCAN_PARTS):
        bu = bu_scr.at[part]
        for tau in range(SUBS_PER_PART * SCAN_SUB):
            rows = slice(tau * SCAN_SEQS, (tau + 1) * SCAN_SEQS)
            n_re = a_re * s_re - a_im * s_im + bu[rows, 0:STATE_LANES]
            n_im = a_re * s_im + a_im * s_re + bu[rows, STATE_LANES:2 * STATE_LANES]
            s_re, s_im = n_re, n_im
            bu[rows, 0:STATE_LANES] = s_re
            bu[rows, STATE_LANES:2 * STATE_LANES] = s_im
        half = SUBS_PER_PART * SCAN_ROWS // 2
        y2 = [jnp.dot(bu[r0:r0 + half, :].astype(BF16), wr_scr[...], preferred_element_type=F32)
              for r0 in (0, half)]
        for q in range(SUBS_PER_PART):
            win_f, win_b = windows(part * SUBS_PER_PART + q)
            r0 = q * SCAN_ROWS % half
            yq = y2[q * SCAN_ROWS // half][r0:r0 + SCAN_ROWS]
            y = jnp.where(is_fwd, yq[:, 0:LANE], yq[:, LANE:2 * LANE])
            yt = jnp.dot(permt_ref[...], y.astype(BF16),
                         preferred_element_type=F32).astype(BF16)
            yf_ref[:, win_f, :] = yt[0:B * SCAN_SUB].reshape(B, SCAN_SUB, LANE)
            yb_ref[:, win_b, :] = yt[B * SCAN_SUB:].reshape(B, SCAN_SUB, LANE)
    sre_scr[...] = s_re
    sim_scr[...] = s_im


def _ssm_scan(h, a_re_t, a_im_t, drive, read, perm, perm_t):
    h3 = h.reshape(B, NZ, D)
    blk = (B, SCAN_CHUNK, LANE)
    fwd_spec = pl.BlockSpec(blk, lambda j, ci: (0, ci, j))
    bwd_spec = pl.BlockSpec(blk, lambda j, ci: (0, _bwd_chunk(ci), j))
    a_spec = pl.BlockSpec((1, SCAN_SEQS, STATE_LANES), lambda j, ci: (j, 0, 0))
    p_spec = pl.BlockSpec((SCAN_ROWS, SCAN_ROWS), lambda j, ci: (0, 0))
    yf, yb = pl.pallas_call(
        _scan_kernel,
        grid=(N_LANE_BLOCKS, N_SCAN_CHUNKS),
        in_specs=[fwd_spec, bwd_spec, a_spec, a_spec,
                  pl.BlockSpec((1, 2, 2 * LANE, LANE), lambda j, ci: (j, 0, 0, 0)),
                  pl.BlockSpec((1, 2, SSM_STATE, 2 * LANE), lambda j, ci: (j, 0, 0, 0)),
                  p_spec, p_spec],
        out_specs=[fwd_spec, bwd_spec],
        out_shape=[jax.ShapeDtypeStruct((B, NZ, D), BF16)] * 2,
        scratch_shapes=[pltpu.VMEM((SCAN_SEQS, STATE_LANES), F32),
                        pltpu.VMEM((SCAN_SEQS, STATE_LANES), F32),
                        pltpu.VMEM((SCAN_PARTS, SUBS_PER_PART * SCAN_ROWS, 2 * STATE_LANES), F32),
                        pltpu.VMEM((2 * LANE, 2 * STATE_LANES), BF16),
                        pltpu.VMEM((2 * STATE_LANES, 2 * LANE), BF16)],
        compiler_params=_cparams(("arbitrary", "arbitrary")),
    )(h3, h3, a_re_t, a_im_t, drive, read, perm, perm_t)
    return yf.reshape(T, D), yb.reshape(T, D)


def _ssm_post_kernel(x_ref, nw_ref, shb_ref, shc_ref, scb_ref, scc_ref, d_ref, yf_ref, yb_ref, o_ref):
    is_ctx = _is_ctx_rows(TM_EW)
    h = _norm_mod(x_ref[...], nw_ref[...], _pick(is_ctx, shb_ref, shc_ref), _pick(is_ctx, scb_ref, scc_ref))
    y = d_ref[...] * h + yf_ref[...] + yb_ref[...]
    o_ref[...] = jax.nn.gelu(y).astype(BF16)


def _ssm_post(xs, mods, norm_w, d_skip, yf, yb):
    shb, shc = _mod_specs(0, TM_EW)
    scb, scc = _mod_specs(1, TM_EW)
    row = pl.BlockSpec((TM_EW, D), lambda i: (i, 0))
    vec = pl.BlockSpec((1, D), lambda i: (0, 0))
    return pl.pallas_call(
        _ssm_post_kernel,
        grid=(T // TM_EW,),
        in_specs=[row, vec, shb, shc, scb, scc, vec, row, row],
        out_specs=row,
        out_shape=jax.ShapeDtypeStruct((T, D), BF16),
        compiler_params=_cparams(("arbitrary",)),
    )(xs, norm_w.reshape(1, D), mods, mods, mods, mods, d_skip.reshape(1, D), yf, yb)


def _glu_res_kernel(a_ref, wa_ref, wb_ref, ba_ref, bb_ref, x_ref, gb_ref, gc_ref, o_ref):
    a = a_ref[...]
    za = jnp.dot(a, wa_ref[...].astype(BF16), preferred_element_type=F32) + ba_ref[...]
    zb = jnp.dot(a, wb_ref[...].astype(BF16), preferred_element_type=F32) + bb_ref[...]
    gate = _pick(_is_ctx_rows(TM_MM), gb_ref, gc_ref)
    o_ref[...] = x_ref[...] + gate * (za * jax.nn.sigmoid(zb))


def _glu_res(a, w_glu, layer, b_glu, xs, mods):
    gb, gc = _mod_specs(2, TM_MM, TN_MM)
    nb = D // TN_MM
    return pl.pallas_call(
        _glu_res_kernel,
        grid=(T // TM_MM, nb),
        in_specs=[pl.BlockSpec((TM_MM, D), lambda i, j: (i, 0)),
                  pl.BlockSpec((None, D, TN_MM), lambda i, j: (layer, 0, j)),
                  pl.BlockSpec((None, D, TN_MM), lambda i, j: (layer, 0, j + nb)),
                  pl.BlockSpec((1, TN_MM), lambda i, j: (0, j)),
                  pl.BlockSpec((1, TN_MM), lambda i, j: (0, j + nb)),
                  pl.BlockSpec((TM_MM, TN_MM), lambda i, j: (i, j)),
                  gb, gc],
        out_specs=pl.BlockSpec((TM_MM, TN_MM), lambda i, j: (i, j)),
        out_shape=jax.ShapeDtypeStruct((T, D), F32),
        compiler_params=_cparams(("arbitrary", "arbitrary")),
    )(a, w_glu, w_glu, b_glu.reshape(1, 2 * D), b_glu.reshape(1, 2 * D), xs, mods, mods)


def _tok_rows_load(ref, n):
    return jnp.concatenate([ref[pl.ds(j, n, stride=ROW_PITCH), :] for j in range(ROW_TILE)], axis=1)


def _tok_rows_store(ref, val, n):
    for j in range(ROW_TILE):
        ref[pl.ds(j, n, stride=ROW_TILE), :] = val[:, j * LANE:(j + 1) * LANE]


def _route_kernel(x_ref, nw_ref, shb_ref, shc_ref, scb_ref, scc_ref, wr_ref, br_ref,
                  h_ref, ri_ref, rw_ref, cnt_ref, carry_scr, *, lat_only):
    @pl.when(pl.program_id(0) == 0)
    def _():
        carry_scr[...] = jnp.zeros_like(carry_scr)

    is_ctx = _is_ctx_rows(TM_RT, lat_only)
    h = _norm_mod(x_ref[...], nw_ref[...], _pick(is_ctx, shb_ref, shc_ref), _pick(is_ctx, scb_ref, scc_ref))
    _tok_rows_store(h_ref, h, TM_RT)

    w = wr_ref[...]
    h_hi = h.astype(BF16)
    h_lo = (h - h_hi.astype(F32)).astype(BF16)
    w_hi = w.astype(BF16)
    w_lo = (w - w_hi.astype(F32)).astype(BF16)
    logits = (jnp.dot(h_hi, w_hi, preferred_element_type=F32)
              + jnp.dot(h_hi, w_lo, preferred_element_type=F32)
              + jnp.dot(h_lo, w_hi, preferred_element_type=F32)) + br_ref[...]

    lane = lax.broadcasted_iota(jnp.int32, (TM_RT, ROUTE_LANES), 1).astype(F32)
    big = float(ROUTE_LANES)
    neg = -jnp.inf
    is_g = lane < N_GROUPS
    g_max = jnp.max(jnp.where(is_g, logits, neg), axis=-1, keepdims=True)
    g_sum = jnp.sum(jnp.where(is_g, jnp.exp(logits - g_max), 0.0), axis=-1, keepdims=True)
    g_p = 1.0 / g_sum
    g_idx = jnp.min(jnp.where(is_g, jnp.where(logits == g_max, lane, big), big), axis=-1, keepdims=True)
    lo = N_GROUPS + N_EPG * g_idx
    e_log = jnp.where(lane >= lo, jnp.where(lane < lo + N_EPG, logits, neg), neg)
    e1 = jnp.max(e_log, axis=-1, keepdims=True)
    i1 = jnp.min(jnp.where(e_log == e1, lane, big), axis=-1, keepdims=True)
    e_log2 = jnp.where(lane == i1, neg, e_log)
    e2 = jnp.max(e_log2, axis=-1, keepdims=True)
    i2 = jnp.min(jnp.where(e_log2 == e2, lane, big), axis=-1, keepdims=True)
    p2 = jnp.exp(e2 - e1)
    w1 = g_p / (1.0 + p2)
    w2 = g_p * p2 / (1.0 + p2)
    x1 = i1 - N_GROUPS
    x2 = i2 - N_GROUPS

    sel1 = lane == x1
    sel2 = lane == x2
    onehot = jnp.where(sel1, 1.0, jnp.where(sel2, 1.0, 0.0))
    r_i = lax.broadcasted_iota(jnp.int32, (TM_RT, TM_RT), 0)
    c_i = lax.broadcasted_iota(jnp.int32, (TM_RT, TM_RT), 1)
    tril = jnp.where(r_i > c_i, 1.0, 0.0).astype(BF16)
    before = jnp.dot(tril, onehot.astype(BF16), preferred_element_type=F32) + carry_scr[0:1, :]
    rank1 = jnp.sum(jnp.where(sel1, before, 0.0), axis=-1, keepdims=True)
    rank2 = jnp.sum(jnp.where(sel2, before, 0.0), axis=-1, keepdims=True)
    total = carry_scr[0:1, :] + jnp.sum(onehot, axis=0, keepdims=True)
    carry_scr[...] = jnp.broadcast_to(total, carry_scr.shape)
    cnt_ref[...] = jnp.broadcast_to(total, cnt_ref.shape)

    ri = jnp.where(lane == 0, x1, jnp.where(lane == 1, x2, jnp.where(lane == 2, rank1, jnp.where(lane == 3, rank2, 0.0))))
    ri_ref[...] = ri.astype(jnp.int32)
    rw_ref[...] = jnp.where(lane == 0, w1, jnp.where(lane == 1, w2, 0.0))


def _route(xs, mods, norm_w, w_rg, b_rg, w_re, b_re, lat_only):
    n_tok = B * SEQ if lat_only else T
    pad = ROUTE_LANES - N_GROUPS - N_EXPERTS
    w_cat = jnp.concatenate([w_rg, w_re.reshape(D, N_EXPERTS), jnp.zeros((D, pad), F32)], axis=1)
    b_cat = jnp.concatenate([b_rg, b_re.reshape(N_EXPERTS), jnp.zeros((pad,), F32)]).reshape(1, ROUTE_LANES)
    shb, shc = _mod_specs(3, TM_RT, lat_only=lat_only)
    scb, scc = _mod_specs(4, TM_RT, lat_only=lat_only)
    lanes = pl.BlockSpec((TM_RT, ROUTE_LANES), lambda i: (i, 0))
    return pl.pallas_call(
        functools.partial(_route_kernel, lat_only=lat_only),
        grid=(n_tok // TM_RT,),
        in_specs=[pl.BlockSpec((TM_RT, D), lambda i: (_stream_tile(i, TM_RT, lat_only), 0)),
                  pl.BlockSpec((1, D), lambda i: (0, 0)), shb, shc, scb, scc,
                  pl.BlockSpec((D, ROUTE_LANES), lambda i: (0, 0)),
                  pl.BlockSpec((1, ROUTE_LANES), lambda i: (0, 0))],
        out_specs=[pl.BlockSpec((TM_RT * ROW_TILE, LANE), lambda i: (i, 0)), lanes, lanes,
                   pl.BlockSpec((SUBLANE, ROUTE_LANES), lambda i: (0, 0))],
        out_shape=[jax.ShapeDtypeStruct((n_tok * ROW_TILE, LANE), F32),
                   jax.ShapeDtypeStruct((n_tok, ROUTE_LANES), jnp.int32),
                   jax.ShapeDtypeStruct((n_tok, ROUTE_LANES), F32),
                   jax.ShapeDtypeStruct((SUBLANE, ROUTE_LANES), F32)],
        scratch_shapes=[pltpu.VMEM((SUBLANE, ROUTE_LANES), F32)],
        compiler_params=_cparams(("arbitrary",)),
    )(xs, norm_w.reshape(1, D), mods, mods, mods, mods, w_cat, b_cat)


def _row_gather(src_hbm, off_ref, off_index, dst, sem, n_groups, wait):
    def body(g, c):
        for i in range(GATHER_UNROLL):
            r = g * GATHER_UNROLL + i
            off = 0 if wait else pl.multiple_of(off_ref[off_index(r)], ROW_TILE)
            cp = pltpu.make_async_copy(src_hbm.at[pl.ds(off, ROW_TILE), :],
                                       dst.at[pl.ds(pl.multiple_of(r * ROW_PITCH, SUBLANE), ROW_TILE), :], sem)
            if wait:
                cp.wait()
            else:
                cp.start(priority=i % 2)
        return c

    lax.fori_loop(0, n_groups, body, 0)


def _expert_kernel(be_ref, nv_ref, bv_ref, ro_ref, h_hbm, w1_ref, w3_ref, w2_ref, y_ref, x_scr, sems):
    blk = pl.program_id(0)
    n_valid = nv_ref[0]
    slot = blk % 2

    def gather(block, into, wait):
        groups = (bv_ref[block] + (GATHER_UNROLL - 1)) // GATHER_UNROLL
        _row_gather(h_hbm, ro_ref, lambda r: block * TE + r, x_scr.at[into], sems.at[into], groups, wait)

    @pl.when(blk == 0)
    def _():
        x_scr[...] = jnp.zeros_like(x_scr)
        gather(0, 0, False)

    @pl.when(blk + 1 < n_valid)
    def _():
        gather(blk + 1, 1 - slot, False)

    @pl.when(blk < n_valid)
    def _():
        gather(blk, slot, True)
        x = _tok_rows_load(x_scr.at[slot], TE).astype(BF16)
        a = jnp.dot(x, w1_ref[0].astype(BF16), preferred_element_type=F32)
        c = jnp.dot(x, w3_ref[0].astype(BF16), preferred_element_type=F32)
        mid = (jax.nn.silu(a) * c).astype(BF16)
        _tok_rows_store(y_ref, jnp.dot(mid, w2_ref[0].astype(BF16), preferred_element_type=F32), TE)

    @pl.when(blk >= n_valid)
    def _():
        y_ref[...] = jnp.zeros_like(y_ref)


def _experts(h_rows, block_expert, n_valid, block_rows, row_off, w1, w3, w2):
    grid_spec = pltpu.PrefetchScalarGridSpec(
        num_scalar_prefetch=4,
        grid=(N_EBLOCKS,),
        in_specs=[pl.BlockSpec(memory_space=pl.ANY),
                  pl.BlockSpec((1, D, MOE_F), lambda b, be, nv, bv, ro: (be[b], 0, 0)),
                  pl.BlockSpec((1, D, MOE_F), lambda b, be, nv, bv, ro: (be[b], 0, 0)),
                  pl.BlockSpec((1, MOE_F, D), lambda b, be, nv, bv, ro: (be[b], 0, 0))],
        out_specs=pl.BlockSpec((TE * ROW_TILE, LANE), lambda b, be, nv, bv, ro: (jnp.minimum(b, nv[0]), 0)),
        scratch_shapes=[pltpu.VMEM((2, TE * ROW_PITCH, LANE), F32), pltpu.SemaphoreType.DMA((2,))],
    )
    return pl.pallas_call(
        _expert_kernel,
        grid_spec=grid_spec,
        out_shape=jax.ShapeDtypeStruct((N_EROWS * ROW_TILE, LANE), F32),
        compiler_params=_cparams(("arbitrary",)),
    )(block_expert, n_valid, block_rows, row_off, h_rows,
      w1.reshape(DEPTH * N_EXPERTS, D, MOE_F), w3.reshape(DEPTH * N_EXPERTS, D, MOE_F),
      w2.reshape(DEPTH * N_EXPERTS, MOE_F, D))


def _combine_kernel(do_ref, y_hbm, rw_ref, x_ref, gb_ref, gc_ref, o_ref, buf, sems, *, lat_only):
    tile = pl.program_id(0)
    slot = tile % 2

    def gather(t, into, wait):
        for k in range(MOE_TOPK):
            _row_gather(y_hbm, do_ref, lambda r: (t * TM_RT + r) * MOE_TOPK + k, buf.at[into, k],
                        sems.at[into], TM_RT // GATHER_UNROLL, wait)

    @pl.when(tile == 0)
    def _():
        gather(0, 0, False)

    @pl.when(tile + 1 < pl.num_programs(0))
    def _():
        gather(tile + 1, 1 - slot, False)

    gather(tile, slot, True)
    rw = rw_ref[...]
    y = (rw[:, 0:1] * _tok_rows_load(buf.at[slot, 0], TM_RT)
         + rw[:, 1:2] * _tok_rows_load(buf.at[slot, 1], TM_RT))
    gate = _pick(_is_ctx_rows(TM_RT, lat_only), gb_ref, gc_ref)
    o_ref[...] = x_ref[...] + gate * y


def _combine(ys_rows, dest_off, rw, xs, mods, lat_only):
    n_tok = B * SEQ if lat_only else T
    gb, gc = _mod_specs(5, TM_RT, lat_only=lat_only)
    grid_spec = pltpu.PrefetchScalarGridSpec(
        num_scalar_prefetch=1,
        grid=(n_tok // TM_RT,),
        in_specs=[pl.BlockSpec(memory_space=pl.ANY),
                  pl.BlockSpec((TM_RT, ROUTE_LANES), lambda i, d: (i, 0)),
                  pl.BlockSpec((TM_RT, D), lambda i, d: (_stream_tile(i, TM_RT, lat_only), 0)),
                  gb, gc],
        out_specs=pl.BlockSpec((TM_RT, D), lambda i, d: (i, 0)),
        scratch_shapes=[pltpu.VMEM((2, MOE_TOPK, TM_RT * ROW_PITCH, LANE), F32),
                        pltpu.SemaphoreType.DMA((2,))],
    )
    return pl.pallas_call(
        functools.partial(_combine_kernel, lat_only=lat_only),
        grid_spec=grid_spec,
        out_shape=jax.ShapeDtypeStruct((n_tok, D), F32),
        compiler_params=_cparams(("arbitrary",)),
    )(dest_off, ys_rows, rw, xs, mods, mods)


def _moe(xs, mods, layer, norm_w, w_rg, b_rg, w_re, b_re, w1, w3, w2, lat_only=False):
    n_tok = B * SEQ if lat_only else T
    h_rows, ri, rw, cnt = _route(xs, mods, norm_w, w_rg, b_rg, w_re, b_re, lat_only)
    counts = cnt[0, :N_EXPERTS].astype(jnp.int32)
    padded = (counts + TE - 1) // TE * TE
    pad_end = jnp.cumsum(padded)
    pad_start = pad_end - padded
    dest = (pad_start[ri[:, 0:MOE_TOPK]] + ri[:, MOE_TOPK:2 * MOE_TOPK]).reshape(-1)
    row_off = jnp.zeros((N_EROWS,), jnp.int32).at[dest].set(
        jnp.repeat(jnp.arange(n_tok, dtype=jnp.int32) * ROW_TILE, MOE_TOPK))
    n_valid = pad_end[-1] // TE
    first_row = jnp.minimum(jnp.arange(N_EBLOCKS, dtype=jnp.int32), n_valid - 1) * TE
    block_expert = jnp.minimum(jnp.sum(pad_end[None, :] <= first_row[:, None], axis=1), N_EXPERTS - 1)
    block_rows = jnp.clip((pad_start + counts)[block_expert] - jnp.arange(N_EBLOCKS, dtype=jnp.int32) * TE, 0, TE)
    ys_rows = _experts(h_rows, (block_expert + layer * N_EXPERTS).astype(jnp.int32),
                       n_valid.reshape(1).astype(jnp.int32), block_rows.astype(jnp.int32), row_off, w1, w3, w2)
    return _combine(ys_rows, (dest * ROW_TILE).astype(jnp.int32), rw, xs, mods, lat_only)


def kernel(x, c, ctx, c_ctx, ada_w, ada_b, norm1_w, norm2_w, attn_w_qkv, attn_q_norm, attn_k_norm, attn_lam_q1, attn_lam_k1, attn_lam_q2, attn_lam_k2, attn_subln, attn_w_o, ssm_a_re, ssm_a_im, ssm_log_dt, ssm_b_re, ssm_b_im, ssm_c_re, ssm_c_im, ssm_d, ssm_w_glu, ssm_b_glu, moe_w_rg, moe_b_rg, moe_w_re, moe_b_re, moe_w1, moe_w3, moe_w2):
    xs = jnp.concatenate([ctx, x], axis=1).reshape(T, D)
    mods_all = _ada_table(c, c_ctx, ada_w, ada_b)
    cos, sin = _rope_tables()
    perm, perm_t = _scan_perm()
    for i in range(DEPTH):
        j = i // 2
        mods = mods_all[i].reshape(MOD_ROWS * ADA_CHUNKS, 1, D)
        if i % 2 == 0:
            lam_init = 0.8 - 0.6 * math.exp(-0.3 * i)
            qk, v = _qkv(xs, mods, norm1_w[i], attn_w_qkv, j, attn_q_norm[j], attn_k_norm[j], cos, sin)
            lam_vecs = jnp.stack([attn_lam_q1[j], attn_lam_k1[j], attn_lam_q2[j], attn_lam_k2[j]])
            o = _attention(qk, v, lam_vecs, attn_subln[j], lam_init)
            xs = _proj_res(o, attn_w_o, j, xs, mods, 2)
        else:
            ops = _ssm_operators(ssm_a_re[j], ssm_a_im[j], ssm_log_dt[j], ssm_b_re[j], ssm_b_im[j],
                                 ssm_c_re[j], ssm_c_im[j])
            h = _prenorm(xs, mods, norm1_w[i], 0, BF16)
            yf, yb = _ssm_scan(h, *ops, perm, perm_t)
            g = _ssm_post(xs, mods, norm1_w[i], ssm_d[j], yf, yb)
            xs = _glu_res(g, ssm_w_glu, j, ssm_b_glu[j], xs, mods)
        xs = _moe(xs, mods, i, norm2_w[i], moe_w_rg[i], moe_b_rg[i], moe_w_re[i], moe_b_re[i],
                  moe_w1, moe_w3, moe_w2, lat_only=(i == DEPTH - 1))
    return xs.reshape(B, SEQ, D)
```

```python
import functools
import math

import jax
import jax.numpy as jnp
import numpy as np
from jax import lax
from jax.experimental import pallas as pl
from jax.experimental.pallas import tpu as pltpu

F32 = jnp.float32
BF16 = jnp.bfloat16

D = 2048
B = 4
SEQ = 2048
CTX = 256
NZ = CTX + SEQ
T = B * NZ
DEPTH = 4
GRID_W = 64
NORM_EPS = 1e-6
ADA_CHUNKS = 6
CTX_MOD_ROW = B
MOD_ROWS = 8

HEADS = 8
HEAD_DIM = 128
V_DIM = 2 * HEAD_DIM
QK_WIDTH = HEADS * 2 * HEAD_DIM
DA_SCALE = HEAD_DIM ** -0.5
Q_SCALE = DA_SCALE * math.log2(math.e)
SUBLN_EPS = 1e-5
ROPE_BASE = 10000.0

SSM_CH = 16
SSM_GROUPS = D // SSM_CH
SSM_STATE = 64
LANE = 128
SUBLANE = 8
GROUPS_PER_LANE_BLOCK = LANE // SSM_CH
N_LANE_BLOCKS = D // LANE
STATE_LANES = GROUPS_PER_LANE_BLOCK * SSM_STATE
SCAN_SEQS = 2 * B
SCAN_SUB = 32
SCAN_ROWS = SCAN_SUB * SCAN_SEQS

N_GROUPS = 4
N_EPG = 8
N_EXPERTS = N_GROUPS * N_EPG
MOE_F = 512
MOE_TOPK = 2
ROUTE_LANES = 128
TE = 256
N_EBLOCKS = (T * MOE_TOPK) // TE + N_EXPERTS
N_EROWS = N_EBLOCKS * TE
ROW_TILE = D // LANE
ROW_PITCH = 24
GATHER_UNROLL = 8

TM_MM = 1152
TN_MM = 512
NORM_ROWS = 32
TM_EW = 576
TM_RT = 256
VMEM_LIMIT = 56 * 1024 * 1024


def _cparams(sem):
    return pltpu.CompilerParams(dimension_semantics=sem, vmem_limit_bytes=VMEM_LIMIT)


def _mod_specs(chunk, tm, tn=None, lat_only=False):
    batch = (lambda i: i // (SEQ // tm)) if lat_only else (lambda i: (i * tm) // NZ)
    if tn is None:
        return (pl.BlockSpec((1, 1, D), lambda i, *_: (batch(i) * ADA_CHUNKS + chunk, 0, 0)),
                pl.BlockSpec((1, 1, D), lambda i, *_: (CTX_MOD_ROW * ADA_CHUNKS + chunk, 0, 0)))
    return (pl.BlockSpec((1, 1, tn), lambda i, j: (batch(i) * ADA_CHUNKS + chunk, 0, j)),
            pl.BlockSpec((1, 1, tn), lambda i, j: (CTX_MOD_ROW * ADA_CHUNKS + chunk, 0, j)))


def _is_ctx_rows(tm, lat_only=False, axis=0):
    if lat_only:
        return False
    z0 = (pl.program_id(axis) * tm) % NZ
    return (z0 + lax.broadcasted_iota(jnp.int32, (tm, 1), 0)) < CTX


def _stream_tile(i, tm, lat_only):
    if not lat_only:
        return i
    per_batch = SEQ // tm
    return (i // per_batch) * (NZ // tm) + CTX // tm + i % per_batch


def _pick(is_ctx, b_ref, c_ref):
    return jnp.where(is_ctx, c_ref[0], b_ref[0])


def _norm_mod(x, nw, sh, sc):
    y = x * lax.rsqrt(jnp.mean(x * x, axis=-1, keepdims=True) + NORM_EPS) * nw
    return y * (1.0 + sc) + sh


ADA_TN = 512
ADA_ROWS = B + 1


def _ada_kernel(ct_ref, w_ref, b_ref, o_ref, s_scr):
    @pl.when((pl.program_id(0) == 0) & (pl.program_id(1) == 0))
    def _():
        c = ct_ref[...]
        s = jax.nn.silu(c)
        for r in range(ADA_ROWS):
            s_scr[r] = jnp.broadcast_to(s[:, r:r + 1], (D, LANE))

    nj = ADA_TN // LANE

    def body(kb, accs):
        k0 = pl.multiple_of(kb * SUBLANE, SUBLANE)
        wk = w_ref[0, pl.ds(k0, SUBLANE), :]
        new = []
        for r in range(ADA_ROWS):
            sk = s_scr[r, pl.ds(k0, SUBLANE), :]
            for j in range(nj):
                new.append(accs[r * nj + j] + wk[:, j * LANE:(j + 1) * LANE] * sk)
        return tuple(new)

    zero = jnp.zeros((SUBLANE, LANE), F32)
    accs = lax.fori_loop(0, D // SUBLANE, body, (zero,) * (ADA_ROWS * nj), unroll=4)
    rows = []
    for r in range(ADA_ROWS):
        rows.append(jnp.concatenate(
            [jnp.sum(accs[r * nj + j], axis=0, keepdims=True) for j in range(nj)], axis=1))
    rows.append(jnp.zeros((MOD_ROWS - ADA_ROWS, ADA_TN), F32))
    o_ref[0] = jnp.concatenate(rows, axis=0) + b_ref[0]


def _ada_table(c, c_ctx, ada_w, ada_b):
    cs = jnp.concatenate([c, c_ctx[None, :], jnp.zeros((LANE - ADA_ROWS, D), F32)], axis=0)
    ct = cs.T
    n_out = ADA_CHUNKS * D
    return pl.pallas_call(
        _ada_kernel,
        grid=(DEPTH, n_out // ADA_TN),
        in_specs=[pl.BlockSpec((D, LANE), lambda l, j: (0, 0)),
                  pl.BlockSpec((1, D, ADA_TN), lambda l, j: (l, 0, j)),
                  pl.BlockSpec((1, 1, ADA_TN), lambda l, j: (l, 0, j))],
        out_specs=pl.BlockSpec((1, MOD_ROWS, ADA_TN), lambda l, j: (l, 0, j)),
        out_shape=jax.ShapeDtypeStruct((DEPTH, MOD_ROWS, n_out), F32),
        scratch_shapes=[pltpu.VMEM((ADA_ROWS, D, LANE), F32)],
        compiler_params=_cparams(("arbitrary", "arbitrary")),
    )(ct, ada_w, ada_b.reshape(DEPTH, 1, n_out))


def _rope_tables():
    half = HEAD_DIM // 4
    inv_freq = ROPE_BASE ** (-np.arange(half, dtype=np.float32) / half)
    t = np.arange(SEQ)
    row = (t // GRID_W).astype(np.float32)[:, None] * inv_freq[None, :]
    col = (t % GRID_W).astype(np.float32)[:, None] * inv_freq[None, :]
    cos_l = np.concatenate([np.cos(row), np.cos(row), np.cos(col), np.cos(col)], axis=1)
    sin_l = np.concatenate([-np.sin(row), np.sin(row), -np.sin(col), np.sin(col)], axis=1)
    cos = np.concatenate([np.ones((CTX, HEAD_DIM), np.float32), cos_l.astype(np.float32)], axis=0)
    sin = np.concatenate([np.zeros((CTX, HEAD_DIM), np.float32), sin_l.astype(np.float32)], axis=0)
    return jnp.asarray(cos), jnp.asarray(sin)


TM_QKV = 768
TN_QKV = 1024
TN_HALF = TN_QKV // 2
EP_ROWS = 64


_ROT_PARTNER = np.arange(HEAD_DIM) ^ (HEAD_DIM // 4)


def _head_lane_matrices():
    lanes = np.arange(2 * HEAD_DIM)
    partner = (lanes // HEAD_DIM) * HEAD_DIM + _ROT_PARTNER[lanes % HEAD_DIM]
    swap = (lanes[:, None] == partner[None, :]).astype(np.float32)
    ones = (lanes[:, None] // HEAD_DIM == lanes[None, :] // HEAD_DIM).astype(np.float32)
    return jnp.asarray(swap, BF16), jnp.asarray(ones, BF16)


def _qkv_prologue(x_ref, nw_ref, shb_ref, shc_ref, scb_ref, scc_ref, h_scr):
    @pl.when(pl.program_id(1) == 0)
    def _():
        z0 = (pl.program_id(0) * TM_QKV) % NZ
        for r0 in range(0, TM_QKV, NORM_ROWS):
            rows = slice(r0, r0 + NORM_ROWS)
            is_ctx = z0 + r0 < CTX
            h = _norm_mod(x_ref[rows, :], nw_ref[...], _pick(is_ctx, shb_ref, shc_ref),
                          _pick(is_ctx, scb_ref, scc_ref))
            h_scr[rows, :] = h.astype(BF16)


def _qk_kernel(x_ref, nw_ref, shb_ref, shc_ref, scb_ref, scc_ref, w_ref, qn_ref, kn_ref,
               cos_ref, sin_ref, swap_ref, ones_ref, o_ref, h_scr, acc_scr, rot_scr, ssq_scr):
    _qkv_prologue(x_ref, nw_ref, shb_ref, shc_ref, scb_ref, scc_ref, h_scr)
    is_q = pl.program_id(1) < QK_WIDTH // TN_QKV
    post = jnp.where(is_q, Q_SCALE, 1.0)
    nw = jnp.where(is_q, qn_ref[0:1, :], kn_ref[0:1, :])
    nw_rot = jnp.where(is_q, qn_ref[1:2, :], kn_ref[1:2, :])
    for half in range(2):
        cols = slice(half * TN_HALF, (half + 1) * TN_HALF)
        acc = jnp.dot(h_scr[...], w_ref[:, cols].astype(BF16), preferred_element_type=F32)
        acc_scr[half] = acc
        for blk in range(TN_HALF // (2 * HEAD_DIM)):
            bc = slice(blk * 2 * HEAD_DIM, (blk + 1) * 2 * HEAD_DIM)
            a = acc[:, bc]
            rot_scr[half, :, bc] = jnp.dot(a.astype(BF16), swap_ref[...], preferred_element_type=F32)
            ssq_scr[half, :, bc] = jnp.dot((a * a).astype(BF16), ones_ref[...], preferred_element_type=F32)
        for r0 in range(0, TM_QKV, EP_ROWS):
            rows = slice(r0, r0 + EP_ROWS)
            w_cos = nw * cos_ref[rows, :]
            w_sin = nw_rot * sin_ref[rows, :]
            outs = []
            for c in range(TN_HALF // HEAD_DIM):
                cc = slice(c * HEAD_DIM, (c + 1) * HEAD_DIM)
                scale = lax.rsqrt(ssq_scr[half, rows, cc] * (1.0 / HEAD_DIM) + NORM_EPS) * post
                outs.append(((acc_scr[half, rows, cc] * w_cos + rot_scr[half, rows, cc] * w_sin)
                             * scale).astype(BF16))
            o_ref[rows, half * TN_HALF:(half + 1) * TN_HALF] = jnp.concatenate(outs, axis=1)


def _v_kernel(x_ref, nw_ref, shb_ref, shc_ref, scb_ref, scc_ref, w_ref, o_ref, h_scr):
    _qkv_prologue(x_ref, nw_ref, shb_ref, shc_ref, scb_ref, scc_ref, h_scr)
    for half in range(2):
        cols = slice(half * TN_HALF, (half + 1) * TN_HALF)
        o_ref[:, cols] = jnp.dot(h_scr[...], w_ref[:, cols].astype(BF16),
                                 preferred_element_type=F32).astype(BF16)


def _qkv(xs, mods, norm_w, w_qkv, layer, q_norm, k_norm, cos, sin):
    shb, shc = _mod_specs(0, TM_QKV)
    scb, scc = _mod_specs(1, TM_QKV)
    tiles_per_batch = NZ // TM_QKV
    n_qk = 2 * QK_WIDTH // TN_QKV
    n_v = HEADS * V_DIM // TN_QKV
    row_specs = [pl.BlockSpec((TM_QKV, D), lambda i, j: (i, 0)),
                 pl.BlockSpec((1, D), lambda i, j: (0, 0)), shb, shc, scb, scc]
    row_args = (xs, norm_w.reshape(1, D), mods, mods, mods, mods)
    head_vec = pl.BlockSpec((2, HEAD_DIM), lambda i, j: (0, 0))
    lane_mat = pl.BlockSpec((2 * HEAD_DIM, 2 * HEAD_DIM), lambda i, j: (0, 0))
    swap, ones = _head_lane_matrices()
    ep_scratch = pltpu.VMEM((2, TM_QKV, TN_HALF), F32)
    rope = pl.BlockSpec((TM_QKV, HEAD_DIM), lambda i, j: (i % tiles_per_batch, 0))
    h_scratch = pltpu.VMEM((TM_QKV, D), BF16)
    common = dict(out_specs=pl.BlockSpec((TM_QKV, TN_QKV), lambda i, j: (i, j)),
                  compiler_params=_cparams(("arbitrary", "arbitrary")))
    qk = pl.pallas_call(
        _qk_kernel,
        grid=(T // TM_QKV, n_qk),
        in_specs=row_specs + [pl.BlockSpec((None, D, TN_QKV), lambda i, j: (layer, 0, j)),
                              head_vec, head_vec, rope, rope, lane_mat, lane_mat],
        out_shape=jax.ShapeDtypeStruct((T, 2 * QK_WIDTH), BF16),
        scratch_shapes=[h_scratch, ep_scratch, ep_scratch, ep_scratch], **common,
    )(*row_args, w_qkv, jnp.stack([q_norm, q_norm[_ROT_PARTNER]]), jnp.stack([k_norm, k_norm[_ROT_PARTNER]]),
      cos, sin, swap, ones)
    v = pl.pallas_call(
        _v_kernel,
        grid=(T // TM_QKV, n_v),
        in_specs=row_specs + [pl.BlockSpec((None, D, TN_QKV), lambda i, j: (layer, 0, j + n_qk))],
        out_shape=jax.ShapeDtypeStruct((T, HEADS * V_DIM), BF16), scratch_shapes=[h_scratch], **common,
    )(*row_args, w_qkv)
    return qk, v


TQ = 256
Q_TILES = NZ // TQ


SM_ROWS = 16
HEADS_PER_STEP = 2


def _attn_kernel(lam_ref, q_ref, qn_ref, k_ref, v_ref, sub_ref, o_ref,
                 s_even, s_odd, p_even, p_odd, inv_even, inv_odd, o_scr, *, lam_init):
    lv = lam_ref[...]
    lam = (jnp.exp(jnp.sum(lv[0:1] * lv[1:2], axis=-1, keepdims=True))
           - jnp.exp(jnp.sum(lv[2:3] * lv[3:4], axis=-1, keepdims=True)) + lam_init)

    pairs = [(hh, m) for hh in range(HEADS_PER_STEP) for m in range(2)]

    def scores(q_blk, s_dst, n_keys):
        q = q_blk[...]
        for hh, m in pairs:
            cols = slice((2 * hh + m) * HEAD_DIM, (2 * hh + m + 1) * HEAD_DIM)
            s_dst[hh, m, :, 0:n_keys] = lax.dot_general(q[:, cols], k_ref[0:n_keys, cols], (((1,), (1,)), ((), ())),
                                                        preferred_element_type=F32)

    def softmax(s_src, p_dst, inv_dst, n_keys):
        for hh, m in pairs:
            for r0 in range(0, TQ, SM_ROWS):
                rows = slice(r0, r0 + SM_ROWS)
                s = s_src[hh, m, rows, 0:n_keys]
                p = jnp.exp2(s - jnp.max(s, axis=-1, keepdims=True))
                inv_dst[hh, m, rows, :] = jnp.broadcast_to(1.0 / jnp.sum(p, axis=-1, keepdims=True),
                                                           (SM_ROWS, LANE))
                p_dst[hh, m, rows, 0:n_keys] = p.astype(BF16)

    def values(p_src, inv_src, n_keys):
        for hh, m in pairs:
            o_scr[hh, m] = jnp.dot(p_src[hh, m, :, 0:n_keys], v_ref[0:n_keys, hh * V_DIM:(hh + 1) * V_DIM],
                                   preferred_element_type=F32)
        for hh in range(HEADS_PER_STEP):
            for r0 in range(0, TQ, SM_ROWS):
                rows = slice(r0, r0 + SM_ROWS)
                inv = [jnp.concatenate([inv_src[hh, m, rows, :]] * (V_DIM // LANE), axis=1) for m in range(2)]
                o = o_scr[hh, 0, rows, :] * inv[0] - lam * (o_scr[hh, 1, rows, :] * inv[1])
                o = o * lax.rsqrt(jnp.mean(o * o, axis=-1, keepdims=True) + SUBLN_EPS) * sub_ref[...]
                o_ref[rows, hh * V_DIM:(hh + 1) * V_DIM] = (o * (1.0 - lam_init)).astype(BF16)

    t = pl.program_id(2)

    @pl.when(t == 0)
    def _():
        scores(q_ref, s_even, CTX)
        scores(qn_ref, s_odd, NZ)
        softmax(s_even, p_even, inv_even, CTX)

    @pl.when(t == 1)
    def _():
        values(p_even, inv_even, CTX)
        scores(qn_ref, s_even, NZ)
        softmax(s_odd, p_odd, inv_odd, NZ)

    @pl.when((t > 1) & (t < Q_TILES) & (t % 2 == 0))
    def _():
        values(p_odd, inv_odd, NZ)
        scores(qn_ref, s_odd, NZ)
        softmax(s_even, p_even, inv_even, NZ)

    @pl.when((t > 1) & (t < Q_TILES) & (t % 2 == 1))
    def _():
        values(p_even, inv_even, NZ)
        scores(qn_ref, s_even, NZ)
        softmax(s_odd, p_odd, inv_odd, NZ)

    @pl.when(t == Q_TILES)
    def _():
        values(p_even if (Q_TILES - 1) % 2 == 0 else p_odd, inv_even if (Q_TILES - 1) % 2 == 0 else inv_odd, NZ)


def _attention(qk, v, lam_vecs, subln, lam_init):
    last = Q_TILES - 1
    hps, width = HEADS_PER_STEP, HEADS_PER_STEP * V_DIM
    s_buf = pltpu.VMEM((hps, 2, TQ, NZ), F32)
    p_buf = pltpu.VMEM((hps, 2, TQ, NZ), BF16)
    inv_buf = pltpu.VMEM((hps, 2, TQ, LANE), F32)
    return pl.pallas_call(
        functools.partial(_attn_kernel, lam_init=lam_init),
        grid=(B, HEADS // hps, Q_TILES + 1),
        in_specs=[pl.BlockSpec((4, HEAD_DIM), lambda b, h, t: (0, 0)),
                  pl.BlockSpec((TQ, width), lambda b, h, t: (b * Q_TILES, h)),
                  pl.BlockSpec((TQ, width), lambda b, h, t: (b * Q_TILES + jnp.minimum(t + 1, last), h)),
                  pl.BlockSpec((NZ, width), lambda b, h, t: (b, HEADS // hps + h)),
                  pl.BlockSpec((NZ, width), lambda b, h, t: (b, h)),
                  pl.BlockSpec((1, V_DIM), lambda b, h, t: (0, 0))],
        out_specs=pl.BlockSpec((TQ, width), lambda b, h, t: (b * Q_TILES + jnp.maximum(t - 1, 0), h)),
        out_shape=jax.ShapeDtypeStruct((T, HEADS * V_DIM), BF16),
        scratch_shapes=[s_buf, s_buf, p_buf, p_buf, inv_buf, inv_buf, pltpu.VMEM((hps, 2, TQ, V_DIM), F32)],
        compiler_params=_cparams(("arbitrary", "arbitrary", "arbitrary")),
    )(lam_vecs, qk, qk, qk, v, subln.reshape(1, V_DIM))


def _gate_specs_cols_outer(chunk):
    return (pl.BlockSpec((1, 1, TN_MM), lambda j, i: (((i * TM_MM) // NZ) * ADA_CHUNKS + chunk, 0, j)),
            pl.BlockSpec((1, 1, TN_MM), lambda j, i: (CTX_MOD_ROW * ADA_CHUNKS + chunk, 0, j)))


def _proj_res_kernel(a_ref, w_ref, x_ref, gb_ref, gc_ref, o_ref, w_scr):
    @pl.when(pl.program_id(1) == 0)
    def _():
        w_scr[...] = w_ref[...].astype(BF16)

    acc = jnp.dot(a_ref[...], w_scr[...], preferred_element_type=F32)
    gate = _pick(_is_ctx_rows(TM_MM, axis=1), gb_ref, gc_ref)
    o_ref[...] = x_ref[...] + gate * acc


def _proj_res(a, w, layer, xs, mods, gate_chunk):
    gb, gc = _gate_specs_cols_outer(gate_chunk)
    k = a.shape[1]
    return pl.pallas_call(
        _proj_res_kernel,
        grid=(D // TN_MM, T // TM_MM),
        in_specs=[pl.BlockSpec((TM_MM, k), lambda j, i: (i, 0)),
                  pl.BlockSpec((None, k, TN_MM), lambda j, i: (layer, 0, j)),
                  pl.BlockSpec((TM_MM, TN_MM), lambda j, i: (i, j)),
                  gb, gc],
        out_specs=pl.BlockSpec((TM_MM, TN_MM), lambda j, i: (i, j)),
        out_shape=jax.ShapeDtypeStruct((T, D), F32),
        scratch_shapes=[pltpu.VMEM((k, TN_MM), BF16)],
        compiler_params=_cparams(("arbitrary", "arbitrary")),
    )(a, w, xs, mods, mods)


def _discretize_kernel(are_ref, aim_ref, ldt_ref, bre_ref, bim_ref, abr_ref, abi_ref, bbr_ref, bbi_ref):
    a_re = jnp.minimum(are_ref[...], -1e-4)
    a_im = aim_ref[...]
    dt = jnp.exp(ldt_ref[...])
    mag = jnp.exp(a_re * dt)
    abar_re = mag * jnp.cos(a_im * dt)
    abar_im = mag * jnp.sin(a_im * dt)
    den = a_re * a_re + a_im * a_im
    f_re = ((abar_re - 1.0) * a_re + abar_im * a_im) / den
    f_im = (abar_im * a_re - (abar_re - 1.0) * a_im) / den
    b_re = bre_ref[...]
    b_im = bim_ref[...]
    abr_ref[...] = abar_re
    abi_ref[...] = abar_im
    bbr_ref[...] = f_re * b_re - f_im * b_im
    bbi_ref[...] = f_re * b_im + f_im * b_re


def _ssm_operators(a_re, a_im, log_dt, b_re, b_im, c_re, c_im):
    g, p, ch = SSM_GROUPS, SSM_STATE, SSM_CH
    rows, width = 2 * g, p * ch
    rep = lambda a: jnp.broadcast_to(a[..., None], (2, g, p, ch)).reshape(rows, width)
    spec = pl.BlockSpec((rows, width), lambda: (0, 0))
    abr, abi, bbr, bbi = pl.pallas_call(
        _discretize_kernel,
        in_specs=[spec] * 5,
        out_specs=[spec] * 4,
        out_shape=[jax.ShapeDtypeStruct((rows, width), F32)] * 4,
        compiler_params=pltpu.CompilerParams(vmem_limit_bytes=VMEM_LIMIT),
    )(rep(a_re), rep(a_im), rep(jnp.broadcast_to(log_dt[..., None], (2, g, p))),
      b_re.reshape(rows, width), b_im.reshape(rows, width))
    nj, gl = N_LANE_BLOCKS, GROUPS_PER_LANE_BLOCK

    def a_tiles(a):
        a = a.reshape(2, g, p, ch)[..., 0].reshape(2, nj, STATE_LANES).transpose(1, 0, 2)
        return jnp.repeat(a, B, axis=1)

    bb = jnp.stack([bbr, bbi]).reshape(2, 2, nj, gl, p, ch)
    drive = bb.transpose(2, 0, 1, 3, 5, 4).reshape(nj, 2, 2 * LANE, p)
    drive = jnp.concatenate([drive, drive], axis=-1)
    cc = jnp.stack([c_re, -c_im]).reshape(2, 2, nj, gl, ch, p)
    read = cc.transpose(2, 0, 5, 1, 3, 4).reshape(nj, 2, p, 2 * LANE)
    return a_tiles(abr), a_tiles(abi), drive, read


def _scan_perm():
    perm = np.zeros((SCAN_ROWS, SCAN_ROWS), np.float32)
    for tau in range(SCAN_SUB):
        for s in range(SCAN_SEQS):
            src = tau if s < B else SCAN_SUB - 1 - tau
            perm[tau * SCAN_SEQS + s, s * SCAN_SUB + src] = 1.0
    return jnp.asarray(perm, BF16), jnp.asarray(perm.T, BF16)


def _prenorm_kernel(x_ref, nw_ref, shb_ref, shc_ref, scb_ref, scc_ref, o_ref):
    is_ctx = _is_ctx_rows(TM_EW)
    h = _norm_mod(x_ref[...], nw_ref[...], _pick(is_ctx, shb_ref, shc_ref), _pick(is_ctx, scb_ref, scc_ref))
    o_ref[...] = h.astype(o_ref.dtype)


def _prenorm(xs, mods, norm_w, shift_chunk, dtype):
    shb, shc = _mod_specs(shift_chunk, TM_EW)
    scb, scc = _mod_specs(shift_chunk + 1, TM_EW)
    return pl.pallas_call(
        _prenorm_kernel,
        grid=(T // TM_EW,),
        in_specs=[pl.BlockSpec((TM_EW, D), lambda i: (i, 0)),
                  pl.BlockSpec((1, D), lambda i: (0, 0)), shb, shc, scb, scc],
        out_specs=pl.BlockSpec((TM_EW, D), lambda i: (i, 0)),
        out_shape=jax.ShapeDtypeStruct((T, D), dtype),
        compiler_params=_cparams(("arbitrary",)),
    )(xs, norm_w.reshape(1, D), mods, mods, mods, mods)


SCAN_CHUNK = CTX
N_SCAN_CHUNKS = NZ // SCAN_CHUNK
SCAN_PARTS = 2
SUBS_PER_PART = SCAN_CHUNK // SCAN_SUB // SCAN_PARTS


def _bwd_chunk(ci):
    return jnp.where(ci == 0, 0, N_SCAN_CHUNKS - ci)


LB_PER_STEP = 2


def _scan_kernel(hf_ref, hb_ref, are_ref, aim_ref, drive_ref, read_ref, perm_ref, permt_ref,
                 yf_ref, yb_ref, sre_scr, sim_scr, bu_scr, wd_scr, wr_scr):
    @pl.when(pl.program_id(1) == 0)
    def _():
        sre_scr[...] = jnp.zeros_like(sre_scr)
        sim_scr[...] = jnp.zeros_like(sim_scr)
        gl = GROUPS_PER_LANE_BLOCK
        row_g = (lax.broadcasted_iota(jnp.int32, (2 * LANE, LANE), 0) % LANE) // SSM_CH
        lane_half = lax.broadcasted_iota(jnp.int32, (2 * LANE, LANE), 1) // SSM_STATE
        col_g = (lax.broadcasted_iota(jnp.int32, (SSM_STATE, 2 * LANE), 1) % LANE) // SSM_CH
        for jj in range(LB_PER_STEP):
            for r in range(2):
                for k in range(gl // 2):
                    tile = jnp.where(row_g == 2 * k + lane_half, drive_ref[jj, r], 0.0)
                    v = r * (gl // 2) + k
                    wd_scr[jj, :, v * LANE:(v + 1) * LANE] = tile.astype(BF16)
                for g in range(gl):
                    rows = slice((r * gl + g) * SSM_STATE, (r * gl + g + 1) * SSM_STATE)
                    wr_scr[jj, rows, :] = jnp.where(col_g == g, read_ref[jj, r], 0.0).astype(BF16)

    row = lax.broadcasted_iota(jnp.int32, (SCAN_ROWS, LANE), 0)
    is_fwd = (row % SCAN_SEQS) < B

    def windows(sub):
        off_f = sub * SCAN_SUB
        off_b = SCAN_CHUNK - SCAN_SUB - sub * SCAN_SUB
        return slice(off_f, off_f + SCAN_SUB), slice(off_b, off_b + SCAN_SUB)

    for jj in range(LB_PER_STEP):
        lanes = slice(jj * LANE, (jj + 1) * LANE)
        for part in range(SCAN_PARTS):
            lhs = []
            for q in range(SUBS_PER_PART):
                win_f, win_b = windows(part * SUBS_PER_PART + q)
                win = jnp.concatenate([hf_ref[:, win_f, lanes].reshape(B * SCAN_SUB, LANE),
                                       hb_ref[:, win_b, lanes].reshape(B * SCAN_SUB, LANE)], axis=0)
                u = jnp.dot(perm_ref[...], win, preferred_element_type=F32)
                zero = jnp.zeros_like(u)
                lhs.append(jnp.concatenate([jnp.where(is_fwd, u, zero), jnp.where(is_fwd, zero, u)],
                                           axis=1).astype(BF16))
            bu_scr[jj, part] = jnp.dot(jnp.concatenate(lhs, axis=0), wd_scr[jj], preferred_element_type=F32)

    for jj in range(LB_PER_STEP):
        lanes = slice(jj * LANE, (jj + 1) * LANE)
        a_re = are_ref[jj]
        a_im = aim_ref[jj]
        s_re = sre_scr[jj]
        s_im = sim_scr[jj]
        for part in range(SCAN_PARTS):
            bu = bu_scr.at[jj, part]
            for tau in range(SUBS_PER_PART * SCAN_SUB):
                rows = slice(tau * SCAN_SEQS, (tau + 1) * SCAN_SEQS)
                n_re = a_re * s_re - a_im * s_im + bu[rows, 0:STATE_LANES]
                n_im = a_re * s_im + a_im * s_re + bu[rows, STATE_LANES:2 * STATE_LANES]
                s_re, s_im = n_re, n_im
                bu[rows, 0:STATE_LANES] = s_re
                bu[rows, STATE_LANES:2 * STATE_LANES] = s_im
            half = SUBS_PER_PART * SCAN_ROWS // 2
            y2 = [jnp.dot(bu[r0:r0 + half, :].astype(BF16), wr_scr[jj], preferred_element_type=F32)
                  for r0 in (0, half)]
            for q in range(SUBS_PER_PART):
                win_f, win_b = windows(part * SUBS_PER_PART + q)
                r0 = q * SCAN_ROWS % half
                yq = y2[q * SCAN_ROWS // half][r0:r0 + SCAN_ROWS]
                y = jnp.where(is_fwd, yq[:, 0:LANE], yq[:, LANE:2 * LANE])
                yt = jnp.dot(permt_ref[...], y.astype(BF16),
                             preferred_element_type=F32).astype(BF16)
                yf_ref[:, win_f, lanes] = yt[0:B * SCAN_SUB].reshape(B, SCAN_SUB, LANE)
                yb_ref[:, win_b, lanes] = yt[B * SCAN_SUB:].reshape(B, SCAN_SUB, LANE)
        sre_scr[jj] = s_re
        sim_scr[jj] = s_im


def _ssm_scan(h, a_re_t, a_im_t, drive, read, perm, perm_t):
    h3 = h.reshape(B, NZ, D)
    lb = LB_PER_STEP
    blk = (B, SCAN_CHUNK, lb * LANE)
    fwd_spec = pl.BlockSpec(blk, lambda j, ci: (0, ci, j))
    bwd_spec = pl.BlockSpec(blk, lambda j, ci: (0, _bwd_chunk(ci), j))
    a_spec = pl.BlockSpec((lb, SCAN_SEQS, STATE_LANES), lambda j, ci: (j, 0, 0))
    p_spec = pl.BlockSpec((SCAN_ROWS, SCAN_ROWS), lambda j, ci: (0, 0))
    yf, yb = pl.pallas_call(
        _scan_kernel,
        grid=(N_LANE_BLOCKS // lb, N_SCAN_CHUNKS),
        in_specs=[fwd_spec, bwd_spec, a_spec, a_spec,
                  pl.BlockSpec((lb, 2, 2 * LANE, LANE), lambda j, ci: (j, 0, 0, 0)),
                  pl.BlockSpec((lb, 2, SSM_STATE, 2 * LANE), lambda j, ci: (j, 0, 0, 0)),
                  p_spec, p_spec],
        out_specs=[fwd_spec, bwd_spec],
        out_shape=[jax.ShapeDtypeStruct((B, NZ, D), BF16)] * 2,
        scratch_shapes=[pltpu.VMEM((lb, SCAN_SEQS, STATE_LANES), F32),
                        pltpu.VMEM((lb, SCAN_SEQS, STATE_LANES), F32),
                        pltpu.VMEM((lb, SCAN_PARTS, SUBS_PER_PART * SCAN_ROWS, 2 * STATE_LANES), F32),
                        pltpu.VMEM((lb, 2 * LANE, 2 * STATE_LANES), BF16),
                        pltpu.VMEM((lb, 2 * STATE_LANES, 2 * LANE), BF16)],
        compiler_params=_cparams(("arbitrary", "arbitrary")),
    )(h3, h3, a_re_t, a_im_t, drive, read, perm, perm_t)
    return yf.reshape(T, D), yb.reshape(T, D)


def _ssm_post_kernel(x_ref, nw_ref, shb_ref, shc_ref, scb_ref, scc_ref, d_ref, yf_ref, yb_ref, o_ref):
    is_ctx = _is_ctx_rows(TM_EW)
    h = _norm_mod(x_ref[...], nw_ref[...], _pick(is_ctx, shb_ref, shc_ref), _pick(is_ctx, scb_ref, scc_ref))
    y = d_ref[...] * h + yf_ref[...] + yb_ref[...]
    o_ref[...] = jax.nn.gelu(y).astype(BF16)


def _ssm_post(xs, mods, norm_w, d_skip, yf, yb):
    shb, shc = _mod_specs(0, TM_EW)
    scb, scc = _mod_specs(1, TM_EW)
    row = pl.BlockSpec((TM_EW, D), lambda i: (i, 0))
    vec = pl.BlockSpec((1, D), lambda i: (0, 0))
    return pl.pallas_call(
        _ssm_post_kernel,
        grid=(T // TM_EW,),
        in_specs=[row, vec, shb, shc, scb, scc, vec, row, row],
        out_specs=row,
        out_shape=jax.ShapeDtypeStruct((T, D), BF16),
        compiler_params=_cparams(("arbitrary",)),
    )(xs, norm_w.reshape(1, D), mods, mods, mods, mods, d_skip.reshape(1, D), yf, yb)


def _glu_res_kernel(a_ref, wa_ref, wb_ref, ba_ref, bb_ref, x_ref, gb_ref, gc_ref, o_ref, w_scr):
    @pl.when(pl.program_id(1) == 0)
    def _():
        w_scr[0] = wa_ref[...].astype(BF16)
        w_scr[1] = wb_ref[...].astype(BF16)

    a = a_ref[...]
    za = jnp.dot(a, w_scr[0], preferred_element_type=F32) + ba_ref[...]
    zb = jnp.dot(a, w_scr[1], preferred_element_type=F32) + bb_ref[...]
    gate = _pick(_is_ctx_rows(TM_MM, axis=1), gb_ref, gc_ref)
    o_ref[...] = x_ref[...] + gate * (za * jax.nn.sigmoid(zb))


def _glu_res(a, w_glu, layer, b_glu, xs, mods):
    gb, gc = _gate_specs_cols_outer(2)
    nb = D // TN_MM
    return pl.pallas_call(
        _glu_res_kernel,
        grid=(nb, T // TM_MM),
        in_specs=[pl.BlockSpec((TM_MM, D), lambda j, i: (i, 0)),
                  pl.BlockSpec((None, D, TN_MM), lambda j, i: (layer, 0, j)),
                  pl.BlockSpec((None, D, TN_MM), lambda j, i: (layer, 0, j + nb)),
                  pl.BlockSpec((1, TN_MM), lambda j, i: (0, j)),
                  pl.BlockSpec((1, TN_MM), lambda j, i: (0, j + nb)),
                  pl.BlockSpec((TM_MM, TN_MM), lambda j, i: (i, j)),
                  gb, gc],
        out_specs=pl.BlockSpec((TM_MM, TN_MM), lambda j, i: (i, j)),
        out_shape=jax.ShapeDtypeStruct((T, D), F32),
        scratch_shapes=[pltpu.VMEM((2, D, TN_MM), BF16)],
        compiler_params=_cparams(("arbitrary", "arbitrary")),
    )(a, w_glu, w_glu, b_glu.reshape(1, 2 * D), b_glu.reshape(1, 2 * D), xs, mods, mods)


def _tok_rows_load(ref, n):
    return jnp.concatenate([ref[pl.ds(j, n, stride=ROW_PITCH), :] for j in range(ROW_TILE)], axis=1)


def _tok_rows_store(ref, val, n):
    for j in range(ROW_TILE):
        ref[pl.ds(j, n, stride=ROW_TILE), :] = val[:, j * LANE:(j + 1) * LANE]


def _route_kernel(x_ref, nw_ref, shb_ref, shc_ref, scb_ref, scc_ref, wr_ref, br_ref,
                  h_ref, ri_ref, rw_ref, cnt_ref, carry_scr, *, lat_only):
    @pl.when(pl.program_id(0) == 0)
    def _():
        carry_scr[...] = jnp.zeros_like(carry_scr)

    is_ctx = _is_ctx_rows(TM_RT, lat_only)
    h = _norm_mod(x_ref[...], nw_ref[...], _pick(is_ctx, shb_ref, shc_ref), _pick(is_ctx, scb_ref, scc_ref))
    _tok_rows_store(h_ref, h, TM_RT)

    w = wr_ref[...]
    h_hi = h.astype(BF16)
    h_lo = (h - h_hi.astype(F32)).astype(BF16)
    w_hi = w.astype(BF16)
    w_lo = (w - w_hi.astype(F32)).astype(BF16)
    logits = (jnp.dot(h_hi, w_hi, preferred_element_type=F32)
              + jnp.dot(h_hi, w_lo, preferred_element_type=F32)
              + jnp.dot(h_lo, w_hi, preferred_element_type=F32)) + br_ref[...]

    lane = lax.broadcasted_iota(jnp.int32, (TM_RT, ROUTE_LANES), 1).astype(F32)
    big = float(ROUTE_LANES)
    neg = -jnp.inf
    is_g = lane < N_GROUPS
    g_max = jnp.max(jnp.where(is_g, logits, neg), axis=-1, keepdims=True)
    g_sum = jnp.sum(jnp.where(is_g, jnp.exp(logits - g_max), 0.0), axis=-1, keepdims=True)
    g_p = 1.0 / g_sum
    g_idx = jnp.min(jnp.where(is_g, jnp.where(logits == g_max, lane, big), big), axis=-1, keepdims=True)
    lo = N_GROUPS + N_EPG * g_idx
    e_log = jnp.where(lane >= lo, jnp.where(lane < lo + N_EPG, logits, neg), neg)
    e1 = jnp.max(e_log, axis=-1, keepdims=True)
    i1 = jnp.min(jnp.where(e_log == e1, lane, big), axis=-1, keepdims=True)
    e_log2 = jnp.where(lane == i1, neg, e_log)
    e2 = jnp.max(e_log2, axis=-1, keepdims=True)
    i2 = jnp.min(jnp.where(e_log2 == e2, lane, big), axis=-1, keepdims=True)
    p2 = jnp.exp(e2 - e1)
    w1 = g_p / (1.0 + p2)
    w2 = g_p * p2 / (1.0 + p2)
    x1 = i1 - N_GROUPS
    x2 = i2 - N_GROUPS

    sel1 = lane == x1
    sel2 = lane == x2
    onehot = jnp.where(sel1, 1.0, jnp.where(sel2, 1.0, 0.0))
    r_i = lax.broadcasted_iota(jnp.int32, (TM_RT, TM_RT), 0)
    c_i = lax.broadcasted_iota(jnp.int32, (TM_RT, TM_RT), 1)
    tril = jnp.where(r_i > c_i, 1.0, 0.0).astype(BF16)
    before = jnp.dot(tril, onehot.astype(BF16), preferred_element_type=F32) + carry_scr[0:1, :]
    rank1 = jnp.sum(jnp.where(sel1, before, 0.0), axis=-1, keepdims=True)
    rank2 = jnp.sum(jnp.where(sel2, before, 0.0), axis=-1, keepdims=True)
    total = carry_scr[0:1, :] + jnp.sum(onehot, axis=0, keepdims=True)
    carry_scr[...] = jnp.broadcast_to(total, carry_scr.shape)
    cnt_ref[...] = jnp.broadcast_to(total, cnt_ref.shape)

    ri = jnp.where(lane == 0, x1, jnp.where(lane == 1, x2, jnp.where(lane == 2, rank1, jnp.where(lane == 3, rank2, 0.0))))
    ri_ref[...] = ri.astype(jnp.int32)
    rw_ref[...] = jnp.where(lane == 0, w1, jnp.where(lane == 1, w2, 0.0))


def _route(xs, mods, norm_w, w_rg, b_rg, w_re, b_re, lat_only):
    n_tok = B * SEQ if lat_only else T
    pad = ROUTE_LANES - N_GROUPS - N_EXPERTS
    w_cat = jnp.concatenate([w_rg, w_re.reshape(D, N_EXPERTS), jnp.zeros((D, pad), F32)], axis=1)
    b_cat = jnp.concatenate([b_rg, b_re.reshape(N_EXPERTS), jnp.zeros((pad,), F32)]).reshape(1, ROUTE_LANES)
    shb, shc = _mod_specs(3, TM_RT, lat_only=lat_only)
    scb, scc = _mod_specs(4, TM_RT, lat_only=lat_only)
    lanes = pl.BlockSpec((TM_RT, ROUTE_LANES), lambda i: (i, 0))
    return pl.pallas_call(
        functools.partial(_route_kernel, lat_only=lat_only),
        grid=(n_tok // TM_RT,),
        in_specs=[pl.BlockSpec((TM_RT, D), lambda i: (_stream_tile(i, TM_RT, lat_only), 0)),
                  pl.BlockSpec((1, D), lambda i: (0, 0)), shb, shc, scb, scc,
                  pl.BlockSpec((D, ROUTE_LANES), lambda i: (0, 0)),
                  pl.BlockSpec((1, ROUTE_LANES), lambda i: (0, 0))],
        out_specs=[pl.BlockSpec((TM_RT * ROW_TILE, LANE), lambda i: (i, 0)), lanes, lanes,
                   pl.BlockSpec((SUBLANE, ROUTE_LANES), lambda i: (0, 0))],
        out_shape=[jax.ShapeDtypeStruct((n_tok * ROW_TILE, LANE), F32),
                   jax.ShapeDtypeStruct((n_tok, ROUTE_LANES), jnp.int32),
                   jax.ShapeDtypeStruct((n_tok, ROUTE_LANES), F32),
                   jax.ShapeDtypeStruct((SUBLANE, ROUTE_LANES), F32)],
        scratch_shapes=[pltpu.VMEM((SUBLANE, ROUTE_LANES), F32)],
        compiler_params=_cparams(("arbitrary",)),
    )(xs, norm_w.reshape(1, D), mods, mods, mods, mods, w_cat, b_cat)


def _row_gather(src_hbm, off_ref, off_index, dst, sem, n_groups, wait):
    def body(g, c):
        for i in range(GATHER_UNROLL):
            r = g * GATHER_UNROLL + i
            off = 0 if wait else pl.multiple_of(off_ref[off_index(r)], ROW_TILE)
            cp = pltpu.make_async_copy(src_hbm.at[pl.ds(off, ROW_TILE), :],
                                       dst.at[pl.ds(pl.multiple_of(r * ROW_PITCH, SUBLANE), ROW_TILE), :], sem)
            if wait:
                cp.wait()
            else:
                cp.start(priority=i % 2)
        return c

    lax.fori_loop(0, n_groups, body, 0)


def _expert_kernel(be_ref, nv_ref, bv_ref, ro_ref, h_hbm, w1_ref, w3_ref, w2_ref, y_ref, x_scr, sems):
    blk = pl.program_id(0)
    n_valid = nv_ref[0]
    slot = blk % 2

    def gather(block, into, wait):
        groups = (bv_ref[block] + (GATHER_UNROLL - 1)) // GATHER_UNROLL
        _row_gather(h_hbm, ro_ref, lambda r: block * TE + r, x_scr.at[into], sems.at[into], groups, wait)

    @pl.when(blk == 0)
    def _():
        x_scr[...] = jnp.zeros_like(x_scr)
        gather(0, 0, False)

    @pl.when(blk + 1 < n_valid)
    def _():
        gather(blk + 1, 1 - slot, False)

    @pl.when(blk < n_valid)
    def _():
        gather(blk, slot, True)
        x = _tok_rows_load(x_scr.at[slot], TE).astype(BF16)
        a = jnp.dot(x, w1_ref[0].astype(BF16), preferred_element_type=F32)
        c = jnp.dot(x, w3_ref[0].astype(BF16), preferred_element_type=F32)
        mid = (jax.nn.silu(a) * c).astype(BF16)
        _tok_rows_store(y_ref, jnp.dot(mid, w2_ref[0].astype(BF16), preferred_element_type=F32), TE)

    @pl.when(blk >= n_valid)
    def _():
        y_ref[...] = jnp.zeros_like(y_ref)


def _experts(h_rows, block_expert, n_valid, block_rows, row_off, w1, w3, w2):
    grid_spec = pltpu.PrefetchScalarGridSpec(
        num_scalar_prefetch=4,
        grid=(N_EBLOCKS,),
        in_specs=[pl.BlockSpec(memory_space=pl.ANY),
                  pl.BlockSpec((1, D, MOE_F), lambda b, be, nv, bv, ro: (be[b], 0, 0)),
                  pl.BlockSpec((1, D, MOE_F), lambda b, be, nv, bv, ro: (be[b], 0, 0)),
                  pl.BlockSpec((1, MOE_F, D), lambda b, be, nv, bv, ro: (be[b], 0, 0))],
        out_specs=pl.BlockSpec((TE * ROW_TILE, LANE), lambda b, be, nv, bv, ro: (jnp.minimum(b, nv[0]), 0)),
        scratch_shapes=[pltpu.VMEM((2, TE * ROW_PITCH, LANE), F32), pltpu.SemaphoreType.DMA((2,))],
    )
    return pl.pallas_call(
        _expert_kernel,
        grid_spec=grid_spec,
        out_shape=jax.ShapeDtypeStruct((N_EROWS * ROW_TILE, LANE), F32),
        compiler_params=_cparams(("arbitrary",)),
    )(block_expert, n_valid, block_rows, row_off, h_rows,
      w1.reshape(DEPTH * N_EXPERTS, D, MOE_F), w3.reshape(DEPTH * N_EXPERTS, D, MOE_F),
      w2.reshape(DEPTH * N_EXPERTS, MOE_F, D))


def _combine_kernel(d0_ref, d1_ref, y_hbm, rw_ref, x_ref, gb_ref, gc_ref, o_ref, buf, sems, *, lat_only):
    tile = pl.program_id(0)
    slot = tile % 2

    def gather(t, into, wait):
        for k, d_ref in enumerate((d0_ref, d1_ref)):
            _row_gather(y_hbm, d_ref, lambda r: t * TM_RT + r, buf.at[into, k],
                        sems.at[into], TM_RT // GATHER_UNROLL, wait)

    @pl.when(tile == 0)
    def _():
        gather(0, 0, False)

    @pl.when(tile + 1 < pl.num_programs(0))
    def _():
        gather(tile + 1, 1 - slot, False)

    gather(tile, slot, True)
    rw = rw_ref[...]
    y = (rw[:, 0:1] * _tok_rows_load(buf.at[slot, 0], TM_RT)
         + rw[:, 1:2] * _tok_rows_load(buf.at[slot, 1], TM_RT))
    gate = _pick(_is_ctx_rows(TM_RT, lat_only), gb_ref, gc_ref)
    o_ref[...] = x_ref[...] + gate * y


def _combine(ys_rows, dest_offs, rw, xs, mods, lat_only):
    n_tok = B * SEQ if lat_only else T
    gb, gc = _mod_specs(5, TM_RT, lat_only=lat_only)
    grid_spec = pltpu.PrefetchScalarGridSpec(
        num_scalar_prefetch=MOE_TOPK,
        grid=(n_tok // TM_RT,),
        in_specs=[pl.BlockSpec(memory_space=pl.ANY),
                  pl.BlockSpec((TM_RT, ROUTE_LANES), lambda i, *_: (i, 0)),
                  pl.BlockSpec((TM_RT, D), lambda i, *_: (_stream_tile(i, TM_RT, lat_only), 0)),
                  gb, gc],
        out_specs=pl.BlockSpec((TM_RT, D), lambda i, *_: (i, 0)),
        scratch_shapes=[pltpu.VMEM((2, MOE_TOPK, TM_RT * ROW_PITCH, LANE), F32),
                        pltpu.SemaphoreType.DMA((2,))],
    )
    return pl.pallas_call(
        functools.partial(_combine_kernel, lat_only=lat_only),
        grid_spec=grid_spec,
        out_shape=jax.ShapeDtypeStruct((n_tok, D), F32),
        compiler_params=_cparams(("arbitrary",)),
    )(*dest_offs, ys_rows, rw, xs, mods, mods)


def _row_offsets_kernel(d0_ref, d1_ref, o_ref):
    def clear(i, c):
        o_ref[i] = 0
        return c

    def place(t, c):
        o_ref[d0_ref[t]] = t * ROW_TILE
        o_ref[d1_ref[t]] = t * ROW_TILE
        return c

    lax.fori_loop(0, N_EROWS, clear, 0, unroll=8)
    lax.fori_loop(0, d0_ref.shape[0], place, 0, unroll=8)


def _row_offsets(d0, d1):
    smem = pl.BlockSpec(memory_space=pltpu.SMEM)
    return pl.pallas_call(
        _row_offsets_kernel,
        in_specs=[smem, smem],
        out_specs=smem,
        out_shape=jax.ShapeDtypeStruct((N_EROWS,), jnp.int32),
    )(d0.astype(jnp.int32), d1.astype(jnp.int32))


def _moe(xs, mods, layer, norm_w, w_rg, b_rg, w_re, b_re, w1, w3, w2, lat_only=False):
    n_tok = B * SEQ if lat_only else T
    h_rows, ri, rw, cnt = _route(xs, mods, norm_w, w_rg, b_rg, w_re, b_re, lat_only)
    counts = cnt[0, :N_EXPERTS].astype(jnp.int32)
    padded = (counts + TE - 1) // TE * TE
    pad_end = jnp.cumsum(padded)
    pad_start = pad_end - padded
    dests = [pad_start[ri[:, k]] + ri[:, MOE_TOPK + k] for k in range(MOE_TOPK)]
    row_off = _row_offsets(*dests)
    n_valid = pad_end[-1] // TE
    first_row = jnp.minimum(jnp.arange(N_EBLOCKS, dtype=jnp.int32), n_valid - 1) * TE
    block_expert = jnp.minimum(jnp.sum(pad_end[None, :] <= first_row[:, None], axis=1), N_EXPERTS - 1)
    block_rows = jnp.clip((pad_start + counts)[block_expert] - jnp.arange(N_EBLOCKS, dtype=jnp.int32) * TE, 0, TE)
    ys_rows = _experts(h_rows, (block_expert + layer * N_EXPERTS).astype(jnp.int32),
                       n_valid.reshape(1).astype(jnp.int32), block_rows.astype(jnp.int32), row_off, w1, w3, w2)
    return _combine(ys_rows, [(d * ROW_TILE).astype(jnp.int32) for d in dests], rw, xs, mods, lat_only)


def kernel(x, c, ctx, c_ctx, ada_w, ada_b, norm1_w, norm2_w, attn_w_qkv, attn_q_norm, attn_k_norm, attn_lam_q1, attn_lam_k1, attn_lam_q2, attn_lam_k2, attn_subln, attn_w_o, ssm_a_re, ssm_a_im, ssm_log_dt, ssm_b_re, ssm_b_im, ssm_c_re, ssm_c_im, ssm_d, ssm_w_glu, ssm_b_glu, moe_w_rg, moe_b_rg, moe_w_re, moe_b_re, moe_w1, moe_w3, moe_w2):
    xs = jnp.concatenate([ctx, x], axis=1).reshape(T, D)
    mods_all = _ada_table(c, c_ctx, ada_w, ada_b)
    cos, sin = _rope_tables()
    perm, perm_t = _scan_perm()
    for i in range(DEPTH):
        j = i // 2
        mods = mods_all[i].reshape(MOD_ROWS * ADA_CHUNKS, 1, D)
        if i % 2 == 0:
            lam_init = 0.8 - 0.6 * math.exp(-0.3 * i)
            qk, v = _qkv(xs, mods, norm1_w[i], attn_w_qkv, j, attn_q_norm[j], attn_k_norm[j], cos, sin)
            lam_vecs = jnp.stack([attn_lam_q1[j], attn_lam_k1[j], attn_lam_q2[j], attn_lam_k2[j]])
            o = _attention(qk, v, lam_vecs, attn_subln[j], lam_init)
            xs = _proj_res(o, attn_w_o, j, xs, mods, 2)
        else:
            ops = _ssm_operators(ssm_a_re[j], ssm_a_im[j], ssm_log_dt[j], ssm_b_re[j], ssm_b_im[j],
                                 ssm_c_re[j], ssm_c_im[j])
            h = _prenorm(xs, mods, norm1_w[i], 0, BF16)
            yf, yb = _ssm_scan(h, *ops, perm, perm_t)
            g = _ssm_post(xs, mods, norm1_w[i], ssm_d[j], yf, yb)
            xs = _glu_res(g, ssm_w_glu, j, ssm_b_glu[j], xs, mods)
        xs = _moe(xs, mods, i, norm2_w[i], moe_w_rg[i], moe_b_rg[i], moe_w_re[i], moe_b_re[i],
                  moe_w1, moe_w3, moe_w2, lat_only=(i == DEPTH - 1))
    return xs.reshape(B, SEQ, D)
```

```python
import functools
import math

import jax
import jax.numpy as jnp
import numpy as np
from jax import lax
from jax.experimental import pallas as pl
from jax.experimental.pallas import tpu as pltpu

F32 = jnp.float32
BF16 = jnp.bfloat16

D = 2048
B = 4
SEQ = 2048
CTX = 256
NZ = CTX + SEQ
T = B * NZ
DEPTH = 4
GRID_W = 64
NORM_EPS = 1e-6
ADA_CHUNKS = 6
CTX_MOD_ROW = B
MOD_ROWS = 8

HEADS = 8
HEAD_DIM = 128
V_DIM = 2 * HEAD_DIM
QK_WIDTH = HEADS * 2 * HEAD_DIM
DA_SCALE = HEAD_DIM ** -0.5
Q_SCALE = DA_SCALE * math.log2(math.e)
SUBLN_EPS = 1e-5
ROPE_BASE = 10000.0

SSM_CH = 16
SSM_GROUPS = D // SSM_CH
SSM_STATE = 64
LANE = 128
SUBLANE = 8
GROUPS_PER_LANE_BLOCK = LANE // SSM_CH
N_LANE_BLOCKS = D // LANE
STATE_LANES = GROUPS_PER_LANE_BLOCK * SSM_STATE
SCAN_SEQS = 2 * B
SCAN_SUB = 32
SCAN_ROWS = SCAN_SUB * SCAN_SEQS

N_GROUPS = 4
N_EPG = 8
N_EXPERTS = N_GROUPS * N_EPG
MOE_F = 512
MOE_TOPK = 2
ROUTE_LANES = 128
TE = 256
N_EBLOCKS = (T * MOE_TOPK) // TE + N_EXPERTS
N_EROWS = N_EBLOCKS * TE
ROW_TILE = D // LANE
ROW_PITCH = 24
GATHER_UNROLL = 8

TM_MM = 1152
TN_MM = 512
NORM_ROWS = 32
TM_EW = 576
TM_RT = 256
VMEM_LIMIT = 56 * 1024 * 1024


def _cparams(sem):
    return pltpu.CompilerParams(dimension_semantics=sem, vmem_limit_bytes=VMEM_LIMIT)


def _mod_specs(chunk, tm, tn=None, lat_only=False):
    batch = (lambda i: i // (SEQ // tm)) if lat_only else (lambda i: (i * tm) // NZ)
    if tn is None:
        return (pl.BlockSpec((1, 1, D), lambda i, *_: (batch(i) * ADA_CHUNKS + chunk, 0, 0)),
                pl.BlockSpec((1, 1, D), lambda i, *_: (CTX_MOD_ROW * ADA_CHUNKS + chunk, 0, 0)))
    return (pl.BlockSpec((1, 1, tn), lambda i, j: (batch(i) * ADA_CHUNKS + chunk, 0, j)),
            pl.BlockSpec((1, 1, tn), lambda i, j: (CTX_MOD_ROW * ADA_CHUNKS + chunk, 0, j)))


def _is_ctx_rows(tm, lat_only=False, axis=0):
    if lat_only:
        return False
    z0 = (pl.program_id(axis) * tm) % NZ
    return (z0 + lax.broadcasted_iota(jnp.int32, (tm, 1), 0)) < CTX


def _stream_tile(i, tm, lat_only):
    if not lat_only:
        return i
    per_batch = SEQ // tm
    return (i // per_batch) * (NZ // tm) + CTX // tm + i % per_batch


def _pick(is_ctx, b_ref, c_ref):
    return jnp.where(is_ctx, c_ref[0], b_ref[0])


def _norm_mod(x, nw, sh, sc):
    y = x * lax.rsqrt(jnp.mean(x * x, axis=-1, keepdims=True) + NORM_EPS) * nw
    return y * (1.0 + sc) + sh


ADA_TN = 512
ADA_ROWS = B + 1


def _ada_kernel(ct_ref, w_ref, b_ref, o_ref, s_scr):
    @pl.when((pl.program_id(0) == 0) & (pl.program_id(1) == 0))
    def _():
        c = ct_ref[...]
        s = jax.nn.silu(c)
        for r in range(ADA_ROWS):
            s_scr[r] = jnp.broadcast_to(s[:, r:r + 1], (D, LANE))

    nj = ADA_TN // LANE

    def body(kb, accs):
        k0 = pl.multiple_of(kb * SUBLANE, SUBLANE)
        wk = w_ref[0, pl.ds(k0, SUBLANE), :]
        new = []
        for r in range(ADA_ROWS):
            sk = s_scr[r, pl.ds(k0, SUBLANE), :]
            for j in range(nj):
                new.append(accs[r * nj + j] + wk[:, j * LANE:(j + 1) * LANE] * sk)
        return tuple(new)

    zero = jnp.zeros((SUBLANE, LANE), F32)
    accs = lax.fori_loop(0, D // SUBLANE, body, (zero,) * (ADA_ROWS * nj), unroll=4)
    rows = []
    for r in range(ADA_ROWS):
        rows.append(jnp.concatenate(
            [jnp.sum(accs[r * nj + j], axis=0, keepdims=True) for j in range(nj)], axis=1))
    rows.append(jnp.zeros((MOD_ROWS - ADA_ROWS, ADA_TN), F32))
    o_ref[0] = jnp.concatenate(rows, axis=0) + b_ref[0]


def _ada_table(c, c_ctx, ada_w, ada_b):
    cs = jnp.concatenate([c, c_ctx[None, :], jnp.zeros((LANE - ADA_ROWS, D), F32)], axis=0)
    ct = cs.T
    n_out = ADA_CHUNKS * D
    return pl.pallas_call(
        _ada_kernel,
        grid=(DEPTH, n_out // ADA_TN),
        in_specs=[pl.BlockSpec((D, LANE), lambda l, j: (0, 0)),
                  pl.BlockSpec((1, D, ADA_TN), lambda l, j: (l, 0, j)),
                  pl.BlockSpec((1, 1, ADA_TN), lambda l, j: (l, 0, j))],
        out_specs=pl.BlockSpec((1, MOD_ROWS, ADA_TN), lambda l, j: (l, 0, j)),
        out_shape=jax.ShapeDtypeStruct((DEPTH, MOD_ROWS, n_out), F32),
        scratch_shapes=[pltpu.VMEM((ADA_ROWS, D, LANE), F32)],
        compiler_params=_cparams(("arbitrary", "arbitrary")),
    )(ct, ada_w, ada_b.reshape(DEPTH, 1, n_out))


def _rope_tables():
    half = HEAD_DIM // 4
    inv_freq = ROPE_BASE ** (-np.arange(half, dtype=np.float32) / half)
    t = np.arange(SEQ)
    row = (t // GRID_W).astype(np.float32)[:, None] * inv_freq[None, :]
    col = (t % GRID_W).astype(np.float32)[:, None] * inv_freq[None, :]
    cos_l = np.concatenate([np.cos(row), np.cos(row), np.cos(col), np.cos(col)], axis=1)
    sin_l = np.concatenate([-np.sin(row), np.sin(row), -np.sin(col), np.sin(col)], axis=1)
    cos = np.concatenate([np.ones((CTX, HEAD_DIM), np.float32), cos_l.astype(np.float32)], axis=0)
    sin = np.concatenate([np.zeros((CTX, HEAD_DIM), np.float32), sin_l.astype(np.float32)], axis=0)
    return jnp.asarray(cos), jnp.asarray(sin)


TM_QKV = 768
TN_QKV = 1024
TN_HALF = TN_QKV // 2
EP_ROWS = 64


_ROT_PARTNER = np.arange(HEAD_DIM) ^ (HEAD_DIM // 4)


def _head_lane_matrices():
    lanes = np.arange(2 * HEAD_DIM)
    partner = (lanes // HEAD_DIM) * HEAD_DIM + _ROT_PARTNER[lanes % HEAD_DIM]
    swap = (lanes[:, None] == partner[None, :]).astype(np.float32)
    ones = (lanes[:, None] // HEAD_DIM == lanes[None, :] // HEAD_DIM).astype(np.float32)
    return jnp.asarray(swap, BF16), jnp.asarray(ones, BF16)


def _qkv_prologue(x_ref, nw_ref, shb_ref, shc_ref, scb_ref, scc_ref, h_scr):
    @pl.when(pl.program_id(1) == 0)
    def _():
        z0 = (pl.program_id(0) * TM_QKV) % NZ
        for r0 in range(0, TM_QKV, NORM_ROWS):
            rows = slice(r0, r0 + NORM_ROWS)
            is_ctx = z0 + r0 < CTX
            h = _norm_mod(x_ref[rows, :], nw_ref[...], _pick(is_ctx, shb_ref, shc_ref),
                          _pick(is_ctx, scb_ref, scc_ref))
            h_scr[rows, :] = h.astype(BF16)


def _qk_kernel(x_ref, nw_ref, shb_ref, shc_ref, scb_ref, scc_ref, w_ref, qn_ref, kn_ref,
               cos_ref, sin_ref, swap_ref, ones_ref, o_ref, h_scr, acc_scr, rot_scr, ssq_scr):
    _qkv_prologue(x_ref, nw_ref, shb_ref, shc_ref, scb_ref, scc_ref, h_scr)
    is_q = pl.program_id(1) < QK_WIDTH // TN_QKV
    post = jnp.where(is_q, Q_SCALE, 1.0)
    nw = jnp.where(is_q, qn_ref[0:1, :], kn_ref[0:1, :])
    nw_rot = jnp.where(is_q, qn_ref[1:2, :], kn_ref[1:2, :])
    for half in range(2):
        cols = slice(half * TN_HALF, (half + 1) * TN_HALF)
        acc = jnp.dot(h_scr[...], w_ref[:, cols].astype(BF16), preferred_element_type=F32)
        acc_scr[half] = acc
        for blk in range(TN_HALF // (2 * HEAD_DIM)):
            bc = slice(blk * 2 * HEAD_DIM, (blk + 1) * 2 * HEAD_DIM)
            a = acc[:, bc]
            rot_scr[half, :, bc] = jnp.dot(a.astype(BF16), swap_ref[...], preferred_element_type=F32)
            ssq_scr[half, :, bc] = jnp.dot((a * a).astype(BF16), ones_ref[...], preferred_element_type=F32)
        for r0 in range(0, TM_QKV, EP_ROWS):
            rows = slice(r0, r0 + EP_ROWS)
            w_cos = nw * cos_ref[rows, :]
            w_sin = nw_rot * sin_ref[rows, :]
            outs = []
            for c in range(TN_HALF // HEAD_DIM):
                cc = slice(c * HEAD_DIM, (c + 1) * HEAD_DIM)
                scale = lax.rsqrt(ssq_scr[half, rows, cc] * (1.0 / HEAD_DIM) + NORM_EPS) * post
                outs.append(((acc_scr[half, rows, cc] * w_cos + rot_scr[half, rows, cc] * w_sin)
                             * scale).astype(BF16))
            o_ref[rows, half * TN_HALF:(half + 1) * TN_HALF] = jnp.concatenate(outs, axis=1)


def _v_kernel(x_ref, nw_ref, shb_ref, shc_ref, scb_ref, scc_ref, w_ref, o_ref, h_scr):
    _qkv_prologue(x_ref, nw_ref, shb_ref, shc_ref, scb_ref, scc_ref, h_scr)
    for half in range(2):
        cols = slice(half * TN_HALF, (half + 1) * TN_HALF)
        o_ref[:, cols] = jnp.dot(h_scr[...], w_ref[:, cols].astype(BF16),
                                 preferred_element_type=F32).astype(BF16)


def _qkv(xs, mods, norm_w, w_qkv, layer, q_norm, k_norm, cos, sin):
    shb, shc = _mod_specs(0, TM_QKV)
    scb, scc = _mod_specs(1, TM_QKV)
    tiles_per_batch = NZ // TM_QKV
    n_qk = 2 * QK_WIDTH // TN_QKV
    n_v = HEADS * V_DIM // TN_QKV
    row_specs = [pl.BlockSpec((TM_QKV, D), lambda i, j: (i, 0)),
                 pl.BlockSpec((1, D), lambda i, j: (0, 0)), shb, shc, scb, scc]
    row_args = (xs, norm_w.reshape(1, D), mods, mods, mods, mods)
    head_vec = pl.BlockSpec((2, HEAD_DIM), lambda i, j: (0, 0))
    lane_mat = pl.BlockSpec((2 * HEAD_DIM, 2 * HEAD_DIM), lambda i, j: (0, 0))
    swap, ones = _head_lane_matrices()
    ep_scratch = pltpu.VMEM((2, TM_QKV, TN_HALF), F32)
    rope = pl.BlockSpec((TM_QKV, HEAD_DIM), lambda i, j: (i % tiles_per_batch, 0))
    h_scratch = pltpu.VMEM((TM_QKV, D), BF16)
    common = dict(out_specs=pl.BlockSpec((TM_QKV, TN_QKV), lambda i, j: (i, j)),
                  compiler_params=_cparams(("arbitrary", "arbitrary")))
    qk = pl.pallas_call(
        _qk_kernel,
        grid=(T // TM_QKV, n_qk),
        in_specs=row_specs + [pl.BlockSpec((None, D, TN_QKV), lambda i, j: (layer, 0, j)),
                              head_vec, head_vec, rope, rope, lane_mat, lane_mat],
        out_shape=jax.ShapeDtypeStruct((T, 2 * QK_WIDTH), BF16),
        scratch_shapes=[h_scratch, ep_scratch, ep_scratch, ep_scratch], **common,
    )(*row_args, w_qkv, jnp.stack([q_norm, q_norm[_ROT_PARTNER]]), jnp.stack([k_norm, k_norm[_ROT_PARTNER]]),
      cos, sin, swap, ones)
    v = pl.pallas_call(
        _v_kernel,
        grid=(T // TM_QKV, n_v),
        in_specs=row_specs + [pl.BlockSpec((None, D, TN_QKV), lambda i, j: (layer, 0, j + n_qk))],
        out_shape=jax.ShapeDtypeStruct((T, HEADS * V_DIM), BF16), scratch_shapes=[h_scratch], **common,
    )(*row_args, w_qkv)
    return qk, v


TQ = 256
Q_TILES = NZ // TQ


SM_ROWS = 16
HEADS_PER_STEP = 2


def _attn_kernel(lam_ref, q_ref, qn_ref, k_ref, v_ref, sub_ref, o_ref,
                 s_even, s_odd, p_even, p_odd, inv_even, inv_odd, o_scr, *, lam_init):
    lv = lam_ref[...]
    lam = (jnp.exp(jnp.sum(lv[0:1] * lv[1:2], axis=-1, keepdims=True))
           - jnp.exp(jnp.sum(lv[2:3] * lv[3:4], axis=-1, keepdims=True)) + lam_init)

    pairs = [(hh, m) for hh in range(HEADS_PER_STEP) for m in range(2)]

    def scores(q_blk, s_dst, n_keys):
        q = q_blk[...]
        for hh, m in pairs:
            cols = slice((2 * hh + m) * HEAD_DIM, (2 * hh + m + 1) * HEAD_DIM)
            s_dst[hh, m, :, 0:n_keys] = lax.dot_general(q[:, cols], k_ref[0:n_keys, cols], (((1,), (1,)), ((), ())),
                                                        preferred_element_type=F32)

    def softmax(s_src, p_dst, inv_dst, n_keys):
        for hh, m in pairs:
            for r0 in range(0, TQ, SM_ROWS):
                rows = slice(r0, r0 + SM_ROWS)
                s = s_src[hh, m, rows, 0:n_keys]
                p = jnp.exp2(s - jnp.max(s, axis=-1, keepdims=True))
                inv_dst[hh, m, rows, :] = jnp.broadcast_to(1.0 / jnp.sum(p, axis=-1, keepdims=True),
                                                           (SM_ROWS, LANE))
                p_dst[hh, m, rows, 0:n_keys] = p.astype(BF16)

    def values(p_src, inv_src, n_keys):
        for hh, m in pairs:
            o_scr[hh, m] = jnp.dot(p_src[hh, m, :, 0:n_keys], v_ref[0:n_keys, hh * V_DIM:(hh + 1) * V_DIM],
                                   preferred_element_type=F32)
        for hh in range(HEADS_PER_STEP):
            for r0 in range(0, TQ, SM_ROWS):
                rows = slice(r0, r0 + SM_ROWS)
                inv = [jnp.concatenate([inv_src[hh, m, rows, :]] * (V_DIM // LANE), axis=1) for m in range(2)]
                o = o_scr[hh, 0, rows, :] * inv[0] - lam * (o_scr[hh, 1, rows, :] * inv[1])
                o = o * lax.rsqrt(jnp.mean(o * o, axis=-1, keepdims=True) + SUBLN_EPS) * sub_ref[...]
                o_ref[rows, hh * V_DIM:(hh + 1) * V_DIM] = (o * (1.0 - lam_init)).astype(BF16)

    t = pl.program_id(2)

    @pl.when(t == 0)
    def _():
        scores(q_ref, s_even, CTX)
        scores(qn_ref, s_odd, NZ)
        softmax(s_even, p_even, inv_even, CTX)

    @pl.when(t == 1)
    def _():
        values(p_even, inv_even, CTX)
        scores(qn_ref, s_even, NZ)
        softmax(s_odd, p_odd, inv_odd, NZ)

    @pl.when((t > 1) & (t < Q_TILES) & (t % 2 == 0))
    def _():
        values(p_odd, inv_odd, NZ)
        scores(qn_ref, s_odd, NZ)
        softmax(s_even, p_even, inv_even, NZ)

    @pl.when((t > 1) & (t < Q_TILES) & (t % 2 == 1))
    def _():
        values(p_even, inv_even, NZ)
        scores(qn_ref, s_even, NZ)
        softmax(s_odd, p_odd, inv_odd, NZ)

    @pl.when(t == Q_TILES)
    def _():
        values(p_even if (Q_TILES - 1) % 2 == 0 else p_odd, inv_even if (Q_TILES - 1) % 2 == 0 else inv_odd, NZ)


def _attention(qk, v, lam_vecs, subln, lam_init):
    last = Q_TILES - 1
    hps, width = HEADS_PER_STEP, HEADS_PER_STEP * V_DIM
    s_buf = pltpu.VMEM((hps, 2, TQ, NZ), F32)
    p_buf = pltpu.VMEM((hps, 2, TQ, NZ), BF16)
    inv_buf = pltpu.VMEM((hps, 2, TQ, LANE), F32)
    return pl.pallas_call(
        functools.partial(_attn_kernel, lam_init=lam_init),
        grid=(B, HEADS // hps, Q_TILES + 1),
        in_specs=[pl.BlockSpec((4, HEAD_DIM), lambda b, h, t: (0, 0)),
                  pl.BlockSpec((TQ, width), lambda b, h, t: (b * Q_TILES, h)),
                  pl.BlockSpec((TQ, width), lambda b, h, t: (b * Q_TILES + jnp.minimum(t + 1, last), h)),
                  pl.BlockSpec((NZ, width), lambda b, h, t: (b, HEADS // hps + h)),
                  pl.BlockSpec((NZ, width), lambda b, h, t: (b, h)),
                  pl.BlockSpec((1, V_DIM), lambda b, h, t: (0, 0))],
        out_specs=pl.BlockSpec((TQ, width), lambda b, h, t: (b * Q_TILES + jnp.maximum(t - 1, 0), h)),
        out_shape=jax.ShapeDtypeStruct((T, HEADS * V_DIM), BF16),
        scratch_shapes=[s_buf, s_buf, p_buf, p_buf, inv_buf, inv_buf, pltpu.VMEM((hps, 2, TQ, V_DIM), F32)],
        compiler_params=_cparams(("arbitrary", "arbitrary", "arbitrary")),
    )(lam_vecs, qk, qk, qk, v, subln.reshape(1, V_DIM))


def _gate_specs_cols_outer(chunk):
    return (pl.BlockSpec((1, 1, TN_MM), lambda j, i: (((i * TM_MM) // NZ) * ADA_CHUNKS + chunk, 0, j)),
            pl.BlockSpec((1, 1, TN_MM), lambda j, i: (CTX_MOD_ROW * ADA_CHUNKS + chunk, 0, j)))


def _proj_res_kernel(a_ref, w_ref, x_ref, gb_ref, gc_ref, o_ref, w_scr):
    @pl.when(pl.program_id(1) == 0)
    def _():
        w_scr[...] = w_ref[...].astype(BF16)

    acc = jnp.dot(a_ref[...], w_scr[...], preferred_element_type=F32)
    gate = _pick(_is_ctx_rows(TM_MM, axis=1), gb_ref, gc_ref)
    o_ref[...] = x_ref[...] + gate * acc


def _proj_res(a, w, layer, xs, mods, gate_chunk):
    gb, gc = _gate_specs_cols_outer(gate_chunk)
    k = a.shape[1]
    return pl.pallas_call(
        _proj_res_kernel,
        grid=(D // TN_MM, T // TM_MM),
        in_specs=[pl.BlockSpec((TM_MM, k), lambda j, i: (i, 0)),
                  pl.BlockSpec((None, k, TN_MM), lambda j, i: (layer, 0, j)),
                  pl.BlockSpec((TM_MM, TN_MM), lambda j, i: (i, j)),
                  gb, gc],
        out_specs=pl.BlockSpec((TM_MM, TN_MM), lambda j, i: (i, j)),
        out_shape=jax.ShapeDtypeStruct((T, D), F32),
        scratch_shapes=[pltpu.VMEM((k, TN_MM), BF16)],
        compiler_params=_cparams(("arbitrary", "arbitrary")),
    )(a, w, xs, mods, mods)


def _discretize_kernel(are_ref, aim_ref, ldt_ref, bre_ref, bim_ref, abr_ref, abi_ref, bbr_ref, bbi_ref):
    a_re = jnp.minimum(are_ref[...], -1e-4)
    a_im = aim_ref[...]
    dt = jnp.exp(ldt_ref[...])
    mag = jnp.exp(a_re * dt)
    abar_re = mag * jnp.cos(a_im * dt)
    abar_im = mag * jnp.sin(a_im * dt)
    den = a_re * a_re + a_im * a_im
    f_re = ((abar_re - 1.0) * a_re + abar_im * a_im) / den
    f_im = (abar_im * a_re - (abar_re - 1.0) * a_im) / den
    b_re = bre_ref[...]
    b_im = bim_ref[...]
    abr_ref[...] = abar_re
    abi_ref[...] = abar_im
    bbr_ref[...] = f_re * b_re - f_im * b_im
    bbi_ref[...] = f_re * b_im + f_im * b_re


def _ssm_operators(a_re, a_im, log_dt, b_re, b_im, c_re, c_im):
    g, p, ch = SSM_GROUPS, SSM_STATE, SSM_CH
    rows, width = 2 * g, p * ch
    rep = lambda a: jnp.broadcast_to(a[..., None], (2, g, p, ch)).reshape(rows, width)
    spec = pl.BlockSpec((rows, width), lambda: (0, 0))
    abr, abi, bbr, bbi = pl.pallas_call(
        _discretize_kernel,
        in_specs=[spec] * 5,
        out_specs=[spec] * 4,
        out_shape=[jax.ShapeDtypeStruct((rows, width), F32)] * 4,
        compiler_params=pltpu.CompilerParams(vmem_limit_bytes=VMEM_LIMIT),
    )(rep(a_re), rep(a_im), rep(jnp.broadcast_to(log_dt[..., None], (2, g, p))),
      b_re.reshape(rows, width), b_im.reshape(rows, width))
    nj, gl = N_LANE_BLOCKS, GROUPS_PER_LANE_BLOCK

    def a_tiles(a):
        a = a.reshape(2, g, p, ch)[..., 0].reshape(2, nj, STATE_LANES).transpose(1, 0, 2)
        return jnp.repeat(a, B, axis=1)

    bb = jnp.stack([bbr, bbi]).reshape(2, 2, nj, gl, p, ch)
    drive = bb.transpose(2, 0, 1, 3, 5, 4).reshape(nj, 2, 2 * LANE, p)
    drive = jnp.concatenate([drive, drive], axis=-1)
    cc = jnp.stack([c_re, -c_im]).reshape(2, 2, nj, gl, ch, p)
    read = cc.transpose(2, 0, 5, 1, 3, 4).reshape(nj, 2, p, 2 * LANE)
    return a_tiles(abr), a_tiles(abi), drive, read


def _scan_perm():
    perm = np.zeros((SCAN_ROWS, SCAN_ROWS), np.float32)
    for tau in range(SCAN_SUB):
        for s in range(SCAN_SEQS):
            src = tau if s < B else SCAN_SUB - 1 - tau
            perm[tau * SCAN_SEQS + s, s * SCAN_SUB + src] = 1.0
    return jnp.asarray(perm, BF16), jnp.asarray(perm.T, BF16)


def _prenorm_kernel(x_ref, nw_ref, shb_ref, shc_ref, scb_ref, scc_ref, o_ref):
    is_ctx = _is_ctx_rows(TM_EW)
    h = _norm_mod(x_ref[...], nw_ref[...], _pick(is_ctx, shb_ref, shc_ref), _pick(is_ctx, scb_ref, scc_ref))
    o_ref[...] = h.astype(o_ref.dtype)


def _prenorm(xs, mods, norm_w, shift_chunk, dtype):
    shb, shc = _mod_specs(shift_chunk, TM_EW)
    scb, scc = _mod_specs(shift_chunk + 1, TM_EW)
    return pl.pallas_call(
        _prenorm_kernel,
        grid=(T // TM_EW,),
        in_specs=[pl.BlockSpec((TM_EW, D), lambda i: (i, 0)),
                  pl.BlockSpec((1, D), lambda i: (0, 0)), shb, shc, scb, scc],
        out_specs=pl.BlockSpec((TM_EW, D), lambda i: (i, 0)),
        out_shape=jax.ShapeDtypeStruct((T, D), dtype),
        compiler_params=_cparams(("arbitrary",)),
    )(xs, norm_w.reshape(1, D), mods, mods, mods, mods)


SCAN_CHUNK = CTX
N_SCAN_CHUNKS = NZ // SCAN_CHUNK
SCAN_PARTS = 2
SUBS_PER_PART = SCAN_CHUNK // SCAN_SUB // SCAN_PARTS


def _bwd_chunk(ci):
    return jnp.where(ci == 0, 0, N_SCAN_CHUNKS - ci)


LB_PER_STEP = 2


def _scan_kernel(hf_ref, hb_ref, are_ref, aim_ref, drive_ref, read_ref, perm_ref, permt_ref,
                 yf_ref, yb_ref, sre_scr, sim_scr, bu_scr, wd_scr, wr_scr):
    @pl.when(pl.program_id(1) == 0)
    def _():
        sre_scr[...] = jnp.zeros_like(sre_scr)
        sim_scr[...] = jnp.zeros_like(sim_scr)
        gl = GROUPS_PER_LANE_BLOCK
        row_g = (lax.broadcasted_iota(jnp.int32, (2 * LANE, LANE), 0) % LANE) // SSM_CH
        lane_half = lax.broadcasted_iota(jnp.int32, (2 * LANE, LANE), 1) // SSM_STATE
        col_g = (lax.broadcasted_iota(jnp.int32, (SSM_STATE, 2 * LANE), 1) % LANE) // SSM_CH
        for jj in range(LB_PER_STEP):
            for r in range(2):
                for k in range(gl // 2):
                    tile = jnp.where(row_g == 2 * k + lane_half, drive_ref[jj, r], 0.0)
                    v = r * (gl // 2) + k
                    wd_scr[jj, :, v * LANE:(v + 1) * LANE] = tile.astype(BF16)
                for g in range(gl):
                    rows = slice((r * gl + g) * SSM_STATE, (r * gl + g + 1) * SSM_STATE)
                    wr_scr[jj, rows, :] = jnp.where(col_g == g, read_ref[jj, r], 0.0).astype(BF16)

    row = lax.broadcasted_iota(jnp.int32, (SCAN_ROWS, LANE), 0)
    is_fwd = (row % SCAN_SEQS) < B

    def windows(sub):
        off_f = sub * SCAN_SUB
        off_b = SCAN_CHUNK - SCAN_SUB - sub * SCAN_SUB
        return slice(off_f, off_f + SCAN_SUB), slice(off_b, off_b + SCAN_SUB)

    for jj in range(LB_PER_STEP):
        lanes = slice(jj * LANE, (jj + 1) * LANE)
        for part in range(SCAN_PARTS):
            lhs = []
            for q in range(SUBS_PER_PART):
                win_f, win_b = windows(part * SUBS_PER_PART + q)
                win = jnp.concatenate([hf_ref[:, win_f, lanes].reshape(B * SCAN_SUB, LANE),
                                       hb_ref[:, win_b, lanes].reshape(B * SCAN_SUB, LANE)], axis=0)
                u = jnp.dot(perm_ref[...], win, preferred_element_type=F32)
                zero = jnp.zeros_like(u)
                lhs.append(jnp.concatenate([jnp.where(is_fwd, u, zero), jnp.where(is_fwd, zero, u)],
                                           axis=1).astype(BF16))
            bu_scr[jj, part] = jnp.dot(jnp.concatenate(lhs, axis=0), wd_scr[jj], preferred_element_type=F32)

    for jj in range(LB_PER_STEP):
        lanes = slice(jj * LANE, (jj + 1) * LANE)
        a_re = are_ref[jj]
        a_im = aim_ref[jj]
        s_re = sre_scr[jj]
        s_im = sim_scr[jj]
        for part in range(SCAN_PARTS):
            bu = bu_scr.at[jj, part]
            for tau in range(SUBS_PER_PART * SCAN_SUB):
                rows = slice(tau * SCAN_SEQS, (tau + 1) * SCAN_SEQS)
                n_re = a_re * s_re - a_im * s_im + bu[rows, 0:STATE_LANES]
                n_im = a_re * s_im + a_im * s_re + bu[rows, STATE_LANES:2 * STATE_LANES]
                s_re, s_im = n_re, n_im
                bu[rows, 0:STATE_LANES] = s_re
                bu[rows, STATE_LANES:2 * STATE_LANES] = s_im
            half = SUBS_PER_PART * SCAN_ROWS // 2
            y2 = [jnp.dot(bu[r0:r0 + half, :].astype(BF16), wr_scr[jj], preferred_element_type=F32)
                  for r0 in (0, half)]
            for q in range(SUBS_PER_PART):
                win_f, win_b = windows(part * SUBS_PER_PART + q)
                r0 = q * SCAN_ROWS % half
                yq = y2[q * SCAN_ROWS // half][r0:r0 + SCAN_ROWS]
                y = jnp.where(is_fwd, yq[:, 0:LANE], yq[:, LANE:2 * LANE])
                yt = jnp.dot(permt_ref[...], y.astype(BF16),
                             preferred_element_type=F32).astype(BF16)
                yf_ref[:, win_f, lanes] = yt[0:B * SCAN_SUB].reshape(B, SCAN_SUB, LANE)
                yb_ref[:, win_b, lanes] = yt[B * SCAN_SUB:].reshape(B, SCAN_SUB, LANE)
        sre_scr[jj] = s_re
        sim_scr[jj] = s_im


def _ssm_scan(h, a_re_t, a_im_t, drive, read, perm, perm_t):
    h3 = h.reshape(B, NZ, D)
    lb = LB_PER_STEP
    blk = (B, SCAN_CHUNK, lb * LANE)
    fwd_spec = pl.BlockSpec(blk, lambda j, ci: (0, ci, j))
    bwd_spec = pl.BlockSpec(blk, lambda j, ci: (0, _bwd_chunk(ci), j))
    a_spec = pl.BlockSpec((lb, SCAN_SEQS, STATE_LANES), lambda j, ci: (j, 0, 0))
    p_spec = pl.BlockSpec((SCAN_ROWS, SCAN_ROWS), lambda j, ci: (0, 0))
    yf, yb = pl.pallas_call(
        _scan_kernel,
        grid=(N_LANE_BLOCKS // lb, N_SCAN_CHUNKS),
        in_specs=[fwd_spec, bwd_spec, a_spec, a_spec,
                  pl.BlockSpec((lb, 2, 2 * LANE, LANE), lambda j, ci: (j, 0, 0, 0)),
                  pl.BlockSpec((lb, 2, SSM_STATE, 2 * LANE), lambda j, ci: (j, 0, 0, 0)),
                  p_spec, p_spec],
        out_specs=[fwd_spec, bwd_spec],
        out_shape=[jax.ShapeDtypeStruct((B, NZ, D), BF16)] * 2,
        scratch_shapes=[pltpu.VMEM((lb, SCAN_SEQS, STATE_LANES), F32),
                        pltpu.VMEM((lb, SCAN_SEQS, STATE_LANES), F32),
                        pltpu.VMEM((lb, SCAN_PARTS, SUBS_PER_PART * SCAN_ROWS, 2 * STATE_LANES), F32),
                        pltpu.VMEM((lb, 2 * LANE, 2 * STATE_LANES), BF16),
                        pltpu.VMEM((lb, 2 * STATE_LANES, 2 * LANE), BF16)],
        compiler_params=_cparams(("arbitrary", "arbitrary")),
    )(h3, h3, a_re_t, a_im_t, drive, read, perm, perm_t)
    return yf.reshape(T, D), yb.reshape(T, D)


def _ssm_post_kernel(x_ref, nw_ref, shb_ref, shc_ref, scb_ref, scc_ref, d_ref, yf_ref, yb_ref, o_ref):
    is_ctx = _is_ctx_rows(TM_EW)
    h = _norm_mod(x_ref[...], nw_ref[...], _pick(is_ctx, shb_ref, shc_ref), _pick(is_ctx, scb_ref, scc_ref))
    y = d_ref[...] * h + yf_ref[...] + yb_ref[...]
    o_ref[...] = jax.nn.gelu(y).astype(BF16)


def _ssm_post(xs, mods, norm_w, d_skip, yf, yb):
    shb, shc = _mod_specs(0, TM_EW)
    scb, scc = _mod_specs(1, TM_EW)
    row = pl.BlockSpec((TM_EW, D), lambda i: (i, 0))
    vec = pl.BlockSpec((1, D), lambda i: (0, 0))
    return pl.pallas_call(
        _ssm_post_kernel,
        grid=(T // TM_EW,),
        in_specs=[row, vec, shb, shc, scb, scc, vec, row, row],
        out_specs=row,
        out_shape=jax.ShapeDtypeStruct((T, D), BF16),
        compiler_params=_cparams(("arbitrary",)),
    )(xs, norm_w.reshape(1, D), mods, mods, mods, mods, d_skip.reshape(1, D), yf, yb)


def _glu_res_kernel(a_ref, wa_ref, wb_ref, ba_ref, bb_ref, x_ref, gb_ref, gc_ref, o_ref, w_scr):
    @pl.when(pl.program_id(1) == 0)
    def _():
        w_scr[0] = wa_ref[...].astype(BF16)
        w_scr[1] = wb_ref[...].astype(BF16)

    a = a_ref[...]
    is_ctx = _is_ctx_rows(TM_MM, axis=1)
    half = TN_MM // 2
    for c0 in (0, half):
        cols = slice(c0, c0 + half)
        za = jnp.dot(a, w_scr[0, :, cols], preferred_element_type=F32) + ba_ref[:, cols]
        zb = jnp.dot(a, w_scr[1, :, cols], preferred_element_type=F32) + bb_ref[:, cols]
        gate = jnp.where(is_ctx, gc_ref[0, :, cols], gb_ref[0, :, cols])
        o_ref[:, cols] = x_ref[:, cols] + gate * (za * jax.nn.sigmoid(zb))


def _glu_res(a, w_glu, layer, b_glu, xs, mods):
    gb, gc = _gate_specs_cols_outer(2)
    nb = D // TN_MM
    return pl.pallas_call(
        _glu_res_kernel,
        grid=(nb, T // TM_MM),
        in_specs=[pl.BlockSpec((TM_MM, D), lambda j, i: (i, 0)),
                  pl.BlockSpec((None, D, TN_MM), lambda j, i: (layer, 0, j)),
                  pl.BlockSpec((None, D, TN_MM), lambda j, i: (layer, 0, j + nb)),
                  pl.BlockSpec((1, TN_MM), lambda j, i: (0, j)),
                  pl.BlockSpec((1, TN_MM), lambda j, i: (0, j + nb)),
                  pl.BlockSpec((TM_MM, TN_MM), lambda j, i: (i, j)),
                  gb, gc],
        out_specs=pl.BlockSpec((TM_MM, TN_MM), lambda j, i: (i, j)),
        out_shape=jax.ShapeDtypeStruct((T, D), F32),
        scratch_shapes=[pltpu.VMEM((2, D, TN_MM), BF16)],
        compiler_params=_cparams(("arbitrary", "arbitrary")),
    )(a, w_glu, w_glu, b_glu.reshape(1, 2 * D), b_glu.reshape(1, 2 * D), xs, mods, mods)


def _tok_rows_load(ref, n):
    return jnp.concatenate([ref[pl.ds(j, n, stride=ROW_PITCH), :] for j in range(ROW_TILE)], axis=1)


def _tok_rows_store(ref, val, n):
    for j in range(ROW_TILE):
        ref[pl.ds(j, n, stride=ROW_TILE), :] = val[:, j * LANE:(j + 1) * LANE]


def _route_kernel(x_ref, nw_ref, shb_ref, shc_ref, scb_ref, scc_ref, wr_ref, br_ref,
                  h_ref, ri_ref, rw_ref, cnt_ref, carry_scr, *, lat_only):
    @pl.when(pl.program_id(0) == 0)
    def _():
        carry_scr[...] = jnp.zeros_like(carry_scr)

    is_ctx = _is_ctx_rows(TM_RT, lat_only)
    h = _norm_mod(x_ref[...], nw_ref[...], _pick(is_ctx, shb_ref, shc_ref), _pick(is_ctx, scb_ref, scc_ref))
    _tok_rows_store(h_ref, h, TM_RT)

    w = wr_ref[...]
    h_hi = h.astype(BF16)
    h_lo = (h - h_hi.astype(F32)).astype(BF16)
    w_hi = w.astype(BF16)
    w_lo = (w - w_hi.astype(F32)).astype(BF16)
    logits = (jnp.dot(h_hi, w_hi, preferred_element_type=F32)
              + jnp.dot(h_hi, w_lo, preferred_element_type=F32)
              + jnp.dot(h_lo, w_hi, preferred_element_type=F32)) + br_ref[...]

    lane = lax.broadcasted_iota(jnp.int32, (TM_RT, ROUTE_LANES), 1).astype(F32)
    big = float(ROUTE_LANES)
    neg = -jnp.inf
    is_g = lane < N_GROUPS
    g_max = jnp.max(jnp.where(is_g, logits, neg), axis=-1, keepdims=True)
    g_sum = jnp.sum(jnp.where(is_g, jnp.exp(logits - g_max), 0.0), axis=-1, keepdims=True)
    g_p = 1.0 / g_sum
    g_idx = jnp.min(jnp.where(is_g, jnp.where(logits == g_max, lane, big), big), axis=-1, keepdims=True)
    lo = N_GROUPS + N_EPG * g_idx
    e_log = jnp.where(lane >= lo, jnp.where(lane < lo + N_EPG, logits, neg), neg)
    e1 = jnp.max(e_log, axis=-1, keepdims=True)
    i1 = jnp.min(jnp.where(e_log == e1, lane, big), axis=-1, keepdims=True)
    e_log2 = jnp.where(lane == i1, neg, e_log)
    e2 = jnp.max(e_log2, axis=-1, keepdims=True)
    i2 = jnp.min(jnp.where(e_log2 == e2, lane, big), axis=-1, keepdims=True)
    p2 = jnp.exp(e2 - e1)
    w1 = g_p / (1.0 + p2)
    w2 = g_p * p2 / (1.0 + p2)
    x1 = i1 - N_GROUPS
    x2 = i2 - N_GROUPS

    sel1 = lane == x1
    sel2 = lane == x2
    onehot = jnp.where(sel1, 1.0, jnp.where(sel2, 1.0, 0.0))
    r_i = lax.broadcasted_iota(jnp.int32, (TM_RT, TM_RT), 0)
    c_i = lax.broadcasted_iota(jnp.int32, (TM_RT, TM_RT), 1)
    tril = jnp.where(r_i > c_i, 1.0, 0.0).astype(BF16)
    before = jnp.dot(tril, onehot.astype(BF16), preferred_element_type=F32) + carry_scr[0:1, :]
    rank1 = jnp.sum(jnp.where(sel1, before, 0.0), axis=-1, keepdims=True)
    rank2 = jnp.sum(jnp.where(sel2, before, 0.0), axis=-1, keepdims=True)
    total = carry_scr[0:1, :] + jnp.sum(onehot, axis=0, keepdims=True)
    carry_scr[...] = jnp.broadcast_to(total, carry_scr.shape)
    cnt_ref[...] = jnp.broadcast_to(total, cnt_ref.shape)

    ri = jnp.where(lane == 0, x1, jnp.where(lane == 1, x2, jnp.where(lane == 2, rank1, jnp.where(lane == 3, rank2, 0.0))))
    ri_ref[...] = ri.astype(jnp.int32)
    rw_ref[...] = jnp.where(lane == 0, w1, jnp.where(lane == 1, w2, 0.0))


def _route(xs, mods, norm_w, w_rg, b_rg, w_re, b_re, lat_only):
    n_tok = B * SEQ if lat_only else T
    pad = ROUTE_LANES - N_GROUPS - N_EXPERTS
    w_cat = jnp.concatenate([w_rg, w_re.reshape(D, N_EXPERTS), jnp.zeros((D, pad), F32)], axis=1)
    b_cat = jnp.concatenate([b_rg, b_re.reshape(N_EXPERTS), jnp.zeros((pad,), F32)]).reshape(1, ROUTE_LANES)
    shb, shc = _mod_specs(3, TM_RT, lat_only=lat_only)
    scb, scc = _mod_specs(4, TM_RT, lat_only=lat_only)
    lanes = pl.BlockSpec((TM_RT, ROUTE_LANES), lambda i: (i, 0))
    return pl.pallas_call(
        functools.partial(_route_kernel, lat_only=lat_only),
        grid=(n_tok // TM_RT,),
        in_specs=[pl.BlockSpec((TM_RT, D), lambda i: (_stream_tile(i, TM_RT, lat_only), 0)),
                  pl.BlockSpec((1, D), lambda i: (0, 0)), shb, shc, scb, scc,
                  pl.BlockSpec((D, ROUTE_LANES), lambda i: (0, 0)),
                  pl.BlockSpec((1, ROUTE_LANES), lambda i: (0, 0))],
        out_specs=[pl.BlockSpec((TM_RT * ROW_TILE, LANE), lambda i: (i, 0)), lanes, lanes,
                   pl.BlockSpec((SUBLANE, ROUTE_LANES), lambda i: (0, 0))],
        out_shape=[jax.ShapeDtypeStruct((n_tok * ROW_TILE, LANE), F32),
                   jax.ShapeDtypeStruct((n_tok, ROUTE_LANES), jnp.int32),
                   jax.ShapeDtypeStruct((n_tok, ROUTE_LANES), F32),
                   jax.ShapeDtypeStruct((SUBLANE, ROUTE_LANES), F32)],
        scratch_shapes=[pltpu.VMEM((SUBLANE, ROUTE_LANES), F32)],
        compiler_params=_cparams(("arbitrary",)),
    )(xs, norm_w.reshape(1, D), mods, mods, mods, mods, w_cat, b_cat)


def _row_gather(src_hbm, off_ref, off_index, dst, sem, n_groups, wait):
    def body(g, c):
        for i in range(GATHER_UNROLL):
            r = g * GATHER_UNROLL + i
            off = 0 if wait else pl.multiple_of(off_ref[off_index(r)], ROW_TILE)
            cp = pltpu.make_async_copy(src_hbm.at[pl.ds(off, ROW_TILE), :],
                                       dst.at[pl.ds(pl.multiple_of(r * ROW_PITCH, SUBLANE), ROW_TILE), :], sem)
            if wait:
                cp.wait()
            else:
                cp.start(priority=i % 2)
        return c

    lax.fori_loop(0, n_groups, body, 0)


def _expert_kernel(be_ref, nv_ref, bv_ref, ro_ref, h_hbm, w1_ref, w3_ref, w2_ref, y_ref, x_scr, sems):
    blk = pl.program_id(0)
    n_valid = nv_ref[0]
    slot = blk % 2

    def gather(block, into, wait):
        groups = (bv_ref[block] + (GATHER_UNROLL - 1)) // GATHER_UNROLL
        _row_gather(h_hbm, ro_ref, lambda r: block * TE + r, x_scr.at[into], sems.at[into], groups, wait)

    @pl.when(blk == 0)
    def _():
        x_scr[...] = jnp.zeros_like(x_scr)
        gather(0, 0, False)

    @pl.when(blk + 1 < n_valid)
    def _():
        gather(blk + 1, 1 - slot, False)

    @pl.when(blk < n_valid)
    def _():
        gather(blk, slot, True)
        x = _tok_rows_load(x_scr.at[slot], TE).astype(BF16)
        a = jnp.dot(x, w1_ref[0].astype(BF16), preferred_element_type=F32)
        c = jnp.dot(x, w3_ref[0].astype(BF16), preferred_element_type=F32)
        mid = (jax.nn.silu(a) * c).astype(BF16)
        _tok_rows_store(y_ref, jnp.dot(mid, w2_ref[0].astype(BF16), preferred_element_type=F32), TE)

    @pl.when(blk >= n_valid)
    def _():
        y_ref[...] = jnp.zeros_like(y_ref)


def _experts(h_rows, block_expert, n_valid, block_rows, row_off, w1, w3, w2):
    grid_spec = pltpu.PrefetchScalarGridSpec(
        num_scalar_prefetch=4,
        grid=(N_EBLOCKS,),
        in_specs=[pl.BlockSpec(memory_space=pl.ANY),
                  pl.BlockSpec((1, D, MOE_F), lambda b, be, nv, bv, ro: (be[b], 0, 0)),
                  pl.BlockSpec((1, D, MOE_F), lambda b, be, nv, bv, ro: (be[b], 0, 0)),
                  pl.BlockSpec((1, MOE_F, D), lambda b, be, nv, bv, ro: (be[b], 0, 0))],
        out_specs=pl.BlockSpec((TE * ROW_TILE, LANE), lambda b, be, nv, bv, ro: (jnp.minimum(b, nv[0]), 0)),
        scratch_shapes=[pltpu.VMEM((2, TE * ROW_PITCH, LANE), F32), pltpu.SemaphoreType.DMA((2,))],
    )
    return pl.pallas_call(
        _expert_kernel,
        grid_spec=grid_spec,
        out_shape=jax.ShapeDtypeStruct((N_EROWS * ROW_TILE, LANE), F32),
        compiler_params=_cparams(("arbitrary",)),
    )(block_expert, n_valid, block_rows, row_off, h_rows,
      w1.reshape(DEPTH * N_EXPERTS, D, MOE_F), w3.reshape(DEPTH * N_EXPERTS, D, MOE_F),
      w2.reshape(DEPTH * N_EXPERTS, MOE_F, D))


def _combine_kernel(do_ref, y_hbm, rw_ref, x_ref, gb_ref, gc_ref, o_ref, buf, sems, *, lat_only):
    tile = pl.program_id(0)
    slot = tile % 2

    def gather(t, into, wait):
        for k in range(MOE_TOPK):
            _row_gather(y_hbm, do_ref, lambda r: (t * TM_RT + r) * MOE_TOPK + k, buf.at[into, k],
                        sems.at[into], TM_RT // GATHER_UNROLL, wait)

    @pl.when(tile == 0)
    def _():
        gather(0, 0, False)

    @pl.when(tile + 1 < pl.num_programs(0))
    def _():
        gather(tile + 1, 1 - slot, False)

    gather(tile, slot, True)
    rw = rw_ref[...]
    y = (rw[:, 0:1] * _tok_rows_load(buf.at[slot, 0], TM_RT)
         + rw[:, 1:2] * _tok_rows_load(buf.at[slot, 1], TM_RT))
    gate = _pick(_is_ctx_rows(TM_RT, lat_only), gb_ref, gc_ref)
    o_ref[...] = x_ref[...] + gate * y


def _combine(ys_rows, dest_off, rw, xs, mods, lat_only):
    n_tok = B * SEQ if lat_only else T
    gb, gc = _mod_specs(5, TM_RT, lat_only=lat_only)
    grid_spec = pltpu.PrefetchScalarGridSpec(
        num_scalar_prefetch=1,
        grid=(n_tok // TM_RT,),
        in_specs=[pl.BlockSpec(memory_space=pl.ANY),
                  pl.BlockSpec((TM_RT, ROUTE_LANES), lambda i, *_: (i, 0)),
                  pl.BlockSpec((TM_RT, D), lambda i, *_: (_stream_tile(i, TM_RT, lat_only), 0)),
                  gb, gc],
        out_specs=pl.BlockSpec((TM_RT, D), lambda i, *_: (i, 0)),
        scratch_shapes=[pltpu.VMEM((2, MOE_TOPK, TM_RT * ROW_PITCH, LANE), F32),
                        pltpu.SemaphoreType.DMA((2,))],
    )
    return pl.pallas_call(
        functools.partial(_combine_kernel, lat_only=lat_only),
        grid_spec=grid_spec,
        out_shape=jax.ShapeDtypeStruct((n_tok, D), F32),
        compiler_params=_cparams(("arbitrary",)),
    )(dest_off, ys_rows, rw, xs, mods, mods)


def _moe(xs, mods, layer, norm_w, w_rg, b_rg, w_re, b_re, w1, w3, w2, lat_only=False):
    n_tok = B * SEQ if lat_only else T
    h_rows, ri, rw, cnt = _route(xs, mods, norm_w, w_rg, b_rg, w_re, b_re, lat_only)
    counts = cnt[0, :N_EXPERTS].astype(jnp.int32)
    padded = (counts + TE - 1) // TE * TE
    pad_end = jnp.cumsum(padded)
    pad_start = pad_end - padded
    dest = (pad_start[ri[:, 0:MOE_TOPK]] + ri[:, MOE_TOPK:2 * MOE_TOPK]).reshape(-1)
    row_off = jnp.zeros((N_EROWS,), jnp.int32).at[dest].set(
        jnp.repeat(jnp.arange(n_tok, dtype=jnp.int32) * ROW_TILE, MOE_TOPK))
    n_valid = pad_end[-1] // TE
    first_row = jnp.minimum(jnp.arange(N_EBLOCKS, dtype=jnp.int32), n_valid - 1) * TE
    block_expert = jnp.minimum(jnp.sum(pad_end[None, :] <= first_row[:, None], axis=1), N_EXPERTS - 1)
    block_rows = jnp.clip((pad_start + counts)[block_expert] - jnp.arange(N_EBLOCKS, dtype=jnp.int32) * TE, 0, TE)
    ys_rows = _experts(h_rows, (block_expert + layer * N_EXPERTS).astype(jnp.int32),
                       n_valid.reshape(1).astype(jnp.int32), block_rows.astype(jnp.int32), row_off, w1, w3, w2)
    return _combine(ys_rows, (dest * ROW_TILE).astype(jnp.int32), rw, xs, mods, lat_only)


def kernel(x, c, ctx, c_ctx, ada_w, ada_b, norm1_w, norm2_w, attn_w_qkv, attn_q_norm, attn_k_norm, attn_lam_q1, attn_lam_k1, attn_lam_q2, attn_lam_k2, attn_subln, attn_w_o, ssm_a_re, ssm_a_im, ssm_log_dt, ssm_b_re, ssm_b_im, ssm_c_re, ssm_c_im, ssm_d, ssm_w_glu, ssm_b_glu, moe_w_rg, moe_b_rg, moe_w_re, moe_b_re, moe_w1, moe_w3, moe_w2):
    xs = jnp.concatenate([ctx, x], axis=1).reshape(T, D)
    mods_all = _ada_table(c, c_ctx, ada_w, ada_b)
    cos, sin = _rope_tables()
    perm, perm_t = _scan_perm()
    for i in range(DEPTH):
        j = i // 2
        mods = mods_all[i].reshape(MOD_ROWS * ADA_CHUNKS, 1, D)
        if i % 2 == 0:
            lam_init = 0.8 - 0.6 * math.exp(-0.3 * i)
            qk, v = _qkv(xs, mods, norm1_w[i], attn_w_qkv, j, attn_q_norm[j], attn_k_norm[j], cos, sin)
            lam_vecs = jnp.stack([attn_lam_q1[j], attn_lam_k1[j], attn_lam_q2[j], attn_lam_k2[j]])
            o = _attention(qk, v, lam_vecs, attn_subln[j], lam_init)
            xs = _proj_res(o, attn_w_o, j, xs, mods, 2)
        else:
            ops = _ssm_operators(ssm_a_re[j], ssm_a_im[j], ssm_log_dt[j], ssm_b_re[j], ssm_b_im[j],
                                 ssm_c_re[j], ssm_c_im[j])
            h = _prenorm(xs, mods, norm1_w[i], 0, BF16)
            yf, yb = _ssm_scan(h, *ops, perm, perm_t)
            g = _ssm_post(xs, mods, norm1_w[i], ssm_d[j], yf, yb)
            xs = _glu_res(g, ssm_w_glu, j, ssm_b_glu[j], xs, mods)
        xs = _moe(xs, mods, i, norm2_w[i], moe_w_rg[i], moe_b_rg[i], moe_w_re[i], moe_b_re[i],
                  moe_w1, moe_w3, moe_w2, lat_only=(i == DEPTH - 1))
    return xs.reshape(B, SEQ, D)
```

```python
import functools
import math

import jax
import jax.numpy as jnp
import numpy as np
from jax import lax
from jax.experimental import pallas as pl
from jax.experimental.pallas import tpu as pltpu

F32 = jnp.float32
BF16 = jnp.bfloat16

D = 2048
B = 4
SEQ = 2048
CTX = 256
NZ = CTX + SEQ
T = B * NZ
DEPTH = 4
GRID_W = 64
NORM_EPS = 1e-6
ADA_CHUNKS = 6
CTX_MOD_ROW = B
MOD_ROWS = 8

HEADS = 8
HEAD_DIM = 128
V_DIM = 2 * HEAD_DIM
QK_WIDTH = HEADS * 2 * HEAD_DIM
DA_SCALE = HEAD_DIM ** -0.5
Q_SCALE = DA_SCALE * math.log2(math.e)
SUBLN_EPS = 1e-5
ROPE_BASE = 10000.0

SSM_CH = 16
SSM_GROUPS = D // SSM_CH
SSM_STATE = 64
LANE = 128
SUBLANE = 8
GROUPS_PER_LANE_BLOCK = LANE // SSM_CH
N_LANE_BLOCKS = D // LANE
STATE_LANES = GROUPS_PER_LANE_BLOCK * SSM_STATE
SCAN_SEQS = 2 * B
SCAN_SUB = 32
SCAN_ROWS = SCAN_SUB * SCAN_SEQS

N_GROUPS = 4
N_EPG = 8
N_EXPERTS = N_GROUPS * N_EPG
MOE_F = 512
MOE_TOPK = 2
ROUTE_LANES = 128
TE = 256
N_EBLOCKS = (T * MOE_TOPK) // TE + N_EXPERTS
N_EROWS = N_EBLOCKS * TE
ROW_TILE = D // LANE
ROW_PITCH = 24
GATHER_UNROLL = 8

TM_MM = 1152
TN_MM = 512
NORM_ROWS = 32
TM_EW = 576
TM_RT = 256
VMEM_LIMIT = 56 * 1024 * 1024


def _cparams(sem):
    return pltpu.CompilerParams(dimension_semantics=sem, vmem_limit_bytes=VMEM_LIMIT)


def _mod_specs(chunk, tm, tn=None, lat_only=False):
    batch = (lambda i: i // (SEQ // tm)) if lat_only else (lambda i: (i * tm) // NZ)
    if tn is None:
        return (pl.BlockSpec((1, 1, D), lambda i, *_: (batch(i) * ADA_CHUNKS + chunk, 0, 0)),
                pl.BlockSpec((1, 1, D), lambda i, *_: (CTX_MOD_ROW * ADA_CHUNKS + chunk, 0, 0)))
    return (pl.BlockSpec((1, 1, tn), lambda i, j: (batch(i) * ADA_CHUNKS + chunk, 0, j)),
            pl.BlockSpec((1, 1, tn), lambda i, j: (CTX_MOD_ROW * ADA_CHUNKS + chunk, 0, j)))


def _is_ctx_rows(tm, lat_only=False, axis=0):
    if lat_only:
        return False
    z0 = (pl.program_id(axis) * tm) % NZ
    return (z0 + lax.broadcasted_iota(jnp.int32, (tm, 1), 0)) < CTX


def _stream_tile(i, tm, lat_only):
    if not lat_only:
        return i
    per_batch = SEQ // tm
    return (i // per_batch) * (NZ // tm) + CTX // tm + i % per_batch


def _pick(is_ctx, b_ref, c_ref):
    return jnp.where(is_ctx, c_ref[0], b_ref[0])


def _norm_mod(x, nw, sh, sc):
    y = x * lax.rsqrt(jnp.mean(x * x, axis=-1, keepdims=True) + NORM_EPS) * nw
    return y * (1.0 + sc) + sh


ADA_TN = 512
ADA_ROWS = B + 1


def _ada_kernel(ct_ref, w_ref, b_ref, o_ref, s_scr):
    @pl.when((pl.program_id(0) == 0) & (pl.program_id(1) == 0))
    def _():
        c = ct_ref[...]
        s = jax.nn.silu(c)
        for r in range(ADA_ROWS):
            s_scr[r] = jnp.broadcast_to(s[:, r:r + 1], (D, LANE))

    nj = ADA_TN // LANE

    def body(kb, accs):
        k0 = pl.multiple_of(kb * SUBLANE, SUBLANE)
        wk = w_ref[0, pl.ds(k0, SUBLANE), :]
        new = []
        for r in range(ADA_ROWS):
            sk = s_scr[r, pl.ds(k0, SUBLANE), :]
            for j in range(nj):
                new.append(accs[r * nj + j] + wk[:, j * LANE:(j + 1) * LANE] * sk)
        return tuple(new)

    zero = jnp.zeros((SUBLANE, LANE), F32)
    accs = lax.fori_loop(0, D // SUBLANE, body, (zero,) * (ADA_ROWS * nj), unroll=4)
    rows = []
    for r in range(ADA_ROWS):
        rows.append(jnp.concatenate(
            [jnp.sum(accs[r * nj + j], axis=0, keepdims=True) for j in range(nj)], axis=1))
    rows.append(jnp.zeros((MOD_ROWS - ADA_ROWS, ADA_TN), F32))
    o_ref[0] = jnp.concatenate(rows, axis=0) + b_ref[0]


def _ada_table(c, c_ctx, ada_w, ada_b):
    cs = jnp.concatenate([c, c_ctx[None, :], jnp.zeros((LANE - ADA_ROWS, D), F32)], axis=0)
    ct = cs.T
    n_out = ADA_CHUNKS * D
    return pl.pallas_call(
        _ada_kernel,
        grid=(DEPTH, n_out // ADA_TN),
        in_specs=[pl.BlockSpec((D, LANE), lambda l, j: (0, 0)),
                  pl.BlockSpec((1, D, ADA_TN), lambda l, j: (l, 0, j)),
                  pl.BlockSpec((1, 1, ADA_TN), lambda l, j: (l, 0, j))],
        out_specs=pl.BlockSpec((1, MOD_ROWS, ADA_TN), lambda l, j: (l, 0, j)),
        out_shape=jax.ShapeDtypeStruct((DEPTH, MOD_ROWS, n_out), F32),
        scratch_shapes=[pltpu.VMEM((ADA_ROWS, D, LANE), F32)],
        compiler_params=_cparams(("arbitrary", "arbitrary")),
    )(ct, ada_w, ada_b.reshape(DEPTH, 1, n_out))


def _rope_tables():
    half = HEAD_DIM // 4
    inv_freq = ROPE_BASE ** (-np.arange(half, dtype=np.float32) / half)
    t = np.arange(SEQ)
    row = (t // GRID_W).astype(np.float32)[:, None] * inv_freq[None, :]
    col = (t % GRID_W).astype(np.float32)[:, None] * inv_freq[None, :]
    cos_l = np.concatenate([np.cos(row), np.cos(row), np.cos(col), np.cos(col)], axis=1)
    sin_l = np.concatenate([-np.sin(row), np.sin(row), -np.sin(col), np.sin(col)], axis=1)
    cos = np.concatenate([np.ones((CTX, HEAD_DIM), np.float32), cos_l.astype(np.float32)], axis=0)
    sin = np.concatenate([np.zeros((CTX, HEAD_DIM), np.float32), sin_l.astype(np.float32)], axis=0)
    return jnp.asarray(cos), jnp.asarray(sin)


TM_QKV = 768
TN_QKV = 1024
TN_HALF = TN_QKV // 2
EP_ROWS = 64


_ROT_PARTNER = np.arange(HEAD_DIM) ^ (HEAD_DIM // 4)


def _head_lane_matrices():
    lanes = np.arange(2 * HEAD_DIM)
    partner = (lanes // HEAD_DIM) * HEAD_DIM + _ROT_PARTNER[lanes % HEAD_DIM]
    swap = (lanes[:, None] == partner[None, :]).astype(np.float32)
    ones = (lanes[:, None] // HEAD_DIM == lanes[None, :] // HEAD_DIM).astype(np.float32)
    return jnp.asarray(swap, BF16), jnp.asarray(ones, BF16)


def _qkv_prologue(x_ref, nw_ref, shb_ref, shc_ref, scb_ref, scc_ref, h_scr):
    @pl.when(pl.program_id(1) == 0)
    def _():
        z0 = (pl.program_id(0) * TM_QKV) % NZ
        for r0 in range(0, TM_QKV, NORM_ROWS):
            rows = slice(r0, r0 + NORM_ROWS)
            is_ctx = z0 + r0 < CTX
            h = _norm_mod(x_ref[rows, :], nw_ref[...], _pick(is_ctx, shb_ref, shc_ref),
                          _pick(is_ctx, scb_ref, scc_ref))
            h_scr[rows, :] = h.astype(BF16)


def _qk_kernel(x_ref, nw_ref, shb_ref, shc_ref, scb_ref, scc_ref, w_ref, qn_ref, kn_ref,
               cos_ref, sin_ref, swap_ref, ones_ref, o_ref, h_scr, acc_scr, rot_scr, ssq_scr):
    _qkv_prologue(x_ref, nw_ref, shb_ref, shc_ref, scb_ref, scc_ref, h_scr)
    is_q = pl.program_id(1) < QK_WIDTH // TN_QKV
    post = jnp.where(is_q, Q_SCALE, 1.0)
    nw = jnp.where(is_q, qn_ref[0:1, :], kn_ref[0:1, :])
    nw_rot = jnp.where(is_q, qn_ref[1:2, :], kn_ref[1:2, :])
    for half in range(2):
        cols = slice(half * TN_HALF, (half + 1) * TN_HALF)
        acc = jnp.dot(h_scr[...], w_ref[:, cols].astype(BF16), preferred_element_type=F32)
        acc_scr[half] = acc
        for blk in range(TN_HALF // (2 * HEAD_DIM)):
            bc = slice(blk * 2 * HEAD_DIM, (blk + 1) * 2 * HEAD_DIM)
            a = acc[:, bc]
            rot_scr[half, :, bc] = jnp.dot(a.astype(BF16), swap_ref[...], preferred_element_type=F32)
            ssq_scr[half, :, bc] = jnp.dot((a * a).astype(BF16), ones_ref[...], preferred_element_type=F32)
        for r0 in range(0, TM_QKV, EP_ROWS):
            rows = slice(r0, r0 + EP_ROWS)
            w_cos = nw * cos_ref[rows, :]
            w_sin = nw_rot * sin_ref[rows, :]
            outs = []
            for c in range(TN_HALF // HEAD_DIM):
                cc = slice(c * HEAD_DIM, (c + 1) * HEAD_DIM)
                scale = lax.rsqrt(ssq_scr[half, rows, cc] * (1.0 / HEAD_DIM) + NORM_EPS) * post
                outs.append(((acc_scr[half, rows, cc] * w_cos + rot_scr[half, rows, cc] * w_sin)
                             * scale).astype(BF16))
            o_ref[rows, half * TN_HALF:(half + 1) * TN_HALF] = jnp.concatenate(outs, axis=1)


def _v_kernel(x_ref, nw_ref, shb_ref, shc_ref, scb_ref, scc_ref, w_ref, o_ref, h_scr):
    _qkv_prologue(x_ref, nw_ref, shb_ref, shc_ref, scb_ref, scc_ref, h_scr)
    for half in range(2):
        cols = slice(half * TN_HALF, (half + 1) * TN_HALF)
        o_ref[:, cols] = jnp.dot(h_scr[...], w_ref[:, cols].astype(BF16),
                                 preferred_element_type=F32).astype(BF16)


def _qkv(xs, mods, norm_w, w_qkv, layer, q_norm, k_norm, cos, sin):
    shb, shc = _mod_specs(0, TM_QKV)
    scb, scc = _mod_specs(1, TM_QKV)
    tiles_per_batch = NZ // TM_QKV
    n_qk = 2 * QK_WIDTH // TN_QKV
    n_v = HEADS * V_DIM // TN_QKV
    row_specs = [pl.BlockSpec((TM_QKV, D), lambda i, j: (i, 0)),
                 pl.BlockSpec((1, D), lambda i, j: (0, 0)), shb, shc, scb, scc]
    row_args = (xs, norm_w.reshape(1, D), mods, mods, mods, mods)
    head_vec = pl.BlockSpec((2, HEAD_DIM), lambda i, j: (0, 0))
    lane_mat = pl.BlockSpec((2 * HEAD_DIM, 2 * HEAD_DIM), lambda i, j: (0, 0))
    swap, ones = _head_lane_matrices()
    ep_scratch = pltpu.VMEM((2, TM_QKV, TN_HALF), F32)
    rope = pl.BlockSpec((TM_QKV, HEAD_DIM), lambda i, j: (i % tiles_per_batch, 0))
    h_scratch = pltpu.VMEM((TM_QKV, D), BF16)
    common = dict(out_specs=pl.BlockSpec((TM_QKV, TN_QKV), lambda i, j: (i, j)),
                  compiler_params=_cparams(("arbitrary", "arbitrary")))
    qk = pl.pallas_call(
        _qk_kernel,
        grid=(T // TM_QKV, n_qk),
        in_specs=row_specs + [pl.BlockSpec((None, D, TN_QKV), lambda i, j: (layer, 0, j)),
                              head_vec, head_vec, rope, rope, lane_mat, lane_mat],
        out_shape=jax.ShapeDtypeStruct((T, 2 * QK_WIDTH), BF16),
        scratch_shapes=[h_scratch, ep_scratch, ep_scratch, ep_scratch], **common,
    )(*row_args, w_qkv, jnp.stack([q_norm, q_norm[_ROT_PARTNER]]), jnp.stack([k_norm, k_norm[_ROT_PARTNER]]),
      cos, sin, swap, ones)
    v = pl.pallas_call(
        _v_kernel,
        grid=(T // TM_QKV, n_v),
        in_specs=row_specs + [pl.BlockSpec((None, D, TN_QKV), lambda i, j: (layer, 0, j + n_qk))],
        out_shape=jax.ShapeDtypeStruct((T, HEADS * V_DIM), BF16), scratch_shapes=[h_scratch], **common,
    )(*row_args, w_qkv)
    return qk, v


TQ = 256
Q_TILES = NZ // TQ


SM_ROWS = 16
HEADS_PER_STEP = 2


def _attn_kernel(lam_ref, q_ref, qn_ref, k_ref, v_ref, sub_ref, o_ref,
                 s_even, s_odd, p_even, p_odd, inv_even, inv_odd, o_scr, *, lam_init):
    lv = lam_ref[...]
    lam = (jnp.exp(jnp.sum(lv[0:1] * lv[1:2], axis=-1, keepdims=True))
           - jnp.exp(jnp.sum(lv[2:3] * lv[3:4], axis=-1, keepdims=True)) + lam_init)

    pairs = [(hh, m) for hh in range(HEADS_PER_STEP) for m in range(2)]

    def scores(q_blk, s_dst, n_keys):
        q = q_blk[...]
        for hh, m in pairs:
            cols = slice((2 * hh + m) * HEAD_DIM, (2 * hh + m + 1) * HEAD_DIM)
            s_dst[hh, m, :, 0:n_keys] = lax.dot_general(q[:, cols], k_ref[0:n_keys, cols], (((1,), (1,)), ((), ())),
                                                        preferred_element_type=F32)

    def softmax(s_src, p_dst, inv_dst, n_keys):
        for hh, m in pairs:
            for r0 in range(0, TQ, SM_ROWS):
                rows = slice(r0, r0 + SM_ROWS)
                s = s_src[hh, m, rows, 0:n_keys]
                p = jnp.exp2(s - jnp.max(s, axis=-1, keepdims=True))
                inv_dst[hh, m, rows, :] = jnp.broadcast_to(1.0 / jnp.sum(p, axis=-1, keepdims=True),
                                                           (SM_ROWS, LANE))
                p_dst[hh, m, rows, 0:n_keys] = p.astype(BF16)

    def values(p_src, inv_src, n_keys):
        for hh, m in pairs:
            o_scr[hh, m] = jnp.dot(p_src[hh, m, :, 0:n_keys], v_ref[0:n_keys, hh * V_DIM:(hh + 1) * V_DIM],
                                   preferred_element_type=F32)
        for hh in range(HEADS_PER_STEP):
            for r0 in range(0, TQ, SM_ROWS):
                rows = slice(r0, r0 + SM_ROWS)
                inv = [jnp.concatenate([inv_src[hh, m, rows, :]] * (V_DIM // LANE), axis=1) for m in range(2)]
                o = o_scr[hh, 0, rows, :] * inv[0] - lam * (o_scr[hh, 1, rows, :] * inv[1])
                o = o * lax.rsqrt(jnp.mean(o * o, axis=-1, keepdims=True) + SUBLN_EPS) * sub_ref[...]
                o_ref[rows, hh * V_DIM:(hh + 1) * V_DIM] = (o * (1.0 - lam_init)).astype(BF16)

    t = pl.program_id(2)

    @pl.when(t == 0)
    def _():
        scores(q_ref, s_even, CTX)
        scores(qn_ref, s_odd, NZ)
        softmax(s_even, p_even, inv_even, CTX)

    @pl.when(t == 1)
    def _():
        values(p_even, inv_even, CTX)
        scores(qn_ref, s_even, NZ)
        softmax(s_odd, p_odd, inv_odd, NZ)

    @pl.when((t > 1) & (t < Q_TILES) & (t % 2 == 0))
    def _():
        values(p_odd, inv_odd, NZ)
        scores(qn_ref, s_odd, NZ)
        softmax(s_even, p_even, inv_even, NZ)

    @pl.when((t > 1) & (t < Q_TILES) & (t % 2 == 1))
    def _():
        values(p_even, inv_even, NZ)
        scores(qn_ref, s_even, NZ)
        softmax(s_odd, p_odd, inv_odd, NZ)

    @pl.when(t == Q_TILES)
    def _():
        values(p_even if (Q_TILES - 1) % 2 == 0 else p_odd, inv_even if (Q_TILES - 1) % 2 == 0 else inv_odd, NZ)


def _attention(qk, v, lam_vecs, subln, lam_init):
    last = Q_TILES - 1
    hps, width = HEADS_PER_STEP, HEADS_PER_STEP * V_DIM
    s_buf = pltpu.VMEM((hps, 2, TQ, NZ), F32)
    p_buf = pltpu.VMEM((hps, 2, TQ, NZ), BF16)
    inv_buf = pltpu.VMEM((hps, 2, TQ, LANE), F32)
    return pl.pallas_call(
        functools.partial(_attn_kernel, lam_init=lam_init),
        grid=(B, HEADS // hps, Q_TILES + 1),
        in_specs=[pl.BlockSpec((4, HEAD_DIM), lambda b, h, t: (0, 0)),
                  pl.BlockSpec((TQ, width), lambda b, h, t: (b * Q_TILES, h)),
                  pl.BlockSpec((TQ, width), lambda b, h, t: (b * Q_TILES + jnp.minimum(t + 1, last), h)),
                  pl.BlockSpec((NZ, width), lambda b, h, t: (b, HEADS // hps + h)),
                  pl.BlockSpec((NZ, width), lambda b, h, t: (b, h)),
                  pl.BlockSpec((1, V_DIM), lambda b, h, t: (0, 0))],
        out_specs=pl.BlockSpec((TQ, width), lambda b, h, t: (b * Q_TILES + jnp.maximum(t - 1, 0), h)),
        out_shape=jax.ShapeDtypeStruct((T, HEADS * V_DIM), BF16),
        scratch_shapes=[s_buf, s_buf, p_buf, p_buf, inv_buf, inv_buf, pltpu.VMEM((hps, 2, TQ, V_DIM), F32)],
        compiler_params=_cparams(("arbitrary", "arbitrary", "arbitrary")),
    )(lam_vecs, qk, qk, qk, v, subln.reshape(1, V_DIM))


def _gate_specs_cols_outer(chunk):
    return (pl.BlockSpec((1, 1, TN_MM), lambda j, i: (((i * TM_MM) // NZ) * ADA_CHUNKS + chunk, 0, j)),
            pl.BlockSpec((1, 1, TN_MM), lambda j, i: (CTX_MOD_ROW * ADA_CHUNKS + chunk, 0, j)))


def _proj_res_kernel(a_ref, w_ref, x_ref, gb_ref, gc_ref, o_ref, w_scr):
    @pl.when(pl.program_id(1) == 0)
    def _():
        w_scr[...] = w_ref[...].astype(BF16)

    acc = jnp.dot(a_ref[...], w_scr[...], preferred_element_type=F32)
    gate = _pick(_is_ctx_rows(TM_MM, axis=1), gb_ref, gc_ref)
    o_ref[...] = x_ref[...] + gate * acc


def _proj_res(a, w, layer, xs, mods, gate_chunk):
    gb, gc = _gate_specs_cols_outer(gate_chunk)
    k = a.shape[1]
    return pl.pallas_call(
        _proj_res_kernel,
        grid=(D // TN_MM, T // TM_MM),
        in_specs=[pl.BlockSpec((TM_MM, k), lambda j, i: (i, 0)),
                  pl.BlockSpec((None, k, TN_MM), lambda j, i: (layer, 0, j)),
                  pl.BlockSpec((TM_MM, TN_MM), lambda j, i: (i, j)),
                  gb, gc],
        out_specs=pl.BlockSpec((TM_MM, TN_MM), lambda j, i: (i, j)),
        out_shape=jax.ShapeDtypeStruct((T, D), F32),
        scratch_shapes=[pltpu.VMEM((k, TN_MM), BF16)],
        compiler_params=_cparams(("arbitrary", "arbitrary")),
    )(a, w, xs, mods, mods)


def _discretize_kernel(are_ref, aim_ref, ldt_ref, bre_ref, bim_ref, abr_ref, abi_ref, bbr_ref, bbi_ref):
    a_re = jnp.minimum(are_ref[...], -1e-4)
    a_im = aim_ref[...]
    dt = jnp.exp(ldt_ref[...])
    mag = jnp.exp(a_re * dt)
    abar_re = mag * jnp.cos(a_im * dt)
    abar_im = mag * jnp.sin(a_im * dt)
    den = a_re * a_re + a_im * a_im
    f_re = ((abar_re - 1.0) * a_re + abar_im * a_im) / den
    f_im = (abar_im * a_re - (abar_re - 1.0) * a_im) / den
    b_re = bre_ref[...]
    b_im = bim_ref[...]
    abr_ref[...] = abar_re
    abi_ref[...] = abar_im
    bbr_ref[...] = f_re * b_re - f_im * b_im
    bbi_ref[...] = f_re * b_im + f_im * b_re


def _ssm_operators(a_re, a_im, log_dt, b_re, b_im, c_re, c_im):
    g, p, ch = SSM_GROUPS, SSM_STATE, SSM_CH
    rows, width = 2 * g, p * ch
    rep = lambda a: jnp.broadcast_to(a[..., None], (2, g, p, ch)).reshape(rows, width)
    spec = pl.BlockSpec((rows, width), lambda: (0, 0))
    abr, abi, bbr, bbi = pl.pallas_call(
        _discretize_kernel,
        in_specs=[spec] * 5,
        out_specs=[spec] * 4,
        out_shape=[jax.ShapeDtypeStruct((rows, width), F32)] * 4,
        compiler_params=pltpu.CompilerParams(vmem_limit_bytes=VMEM_LIMIT),
    )(rep(a_re), rep(a_im), rep(jnp.broadcast_to(log_dt[..., None], (2, g, p))),
      b_re.reshape(rows, width), b_im.reshape(rows, width))
    nj, gl = N_LANE_BLOCKS, GROUPS_PER_LANE_BLOCK

    def a_tiles(a):
        a = a.reshape(2, g, p, ch)[..., 0].reshape(2, nj, STATE_LANES).transpose(1, 0, 2)
        return jnp.repeat(a, B, axis=1)

    bb = jnp.stack([bbr, bbi]).reshape(2, 2, nj, gl, p, ch)
    drive = bb.transpose(2, 0, 1, 3, 5, 4).reshape(nj, 2, 2 * LANE, p)
    drive = jnp.concatenate([drive, drive], axis=-1)
    cc = jnp.stack([c_re, -c_im]).reshape(2, 2, nj, gl, ch, p)
    read = cc.transpose(2, 0, 5, 1, 3, 4).reshape(nj, 2, p, 2 * LANE)
    return a_tiles(abr), a_tiles(abi), drive, read


def _scan_perm():
    perm = np.zeros((SCAN_ROWS, SCAN_ROWS), np.float32)
    for tau in range(SCAN_SUB):
        for s in range(SCAN_SEQS):
            src = tau if s < B else SCAN_SUB - 1 - tau
            perm[tau * SCAN_SEQS + s, s * SCAN_SUB + src] = 1.0
    return jnp.asarray(perm, BF16), jnp.asarray(perm.T, BF16)


def _row_chunks(tm):
    z0 = (pl.program_id(0) * tm) % NZ
    return [(slice(r0, r0 + NORM_ROWS), z0 + r0 < CTX) for r0 in range(0, tm, NORM_ROWS)]


def _prenorm_kernel(x_ref, nw_ref, shb_ref, shc_ref, scb_ref, scc_ref, o_ref):
    for rows, is_ctx in _row_chunks(TM_EW):
        h = _norm_mod(x_ref[rows, :], nw_ref[...], _pick(is_ctx, shb_ref, shc_ref), _pick(is_ctx, scb_ref, scc_ref))
        o_ref[rows, :] = h.astype(o_ref.dtype)


def _prenorm(xs, mods, norm_w, shift_chunk, dtype):
    shb, shc = _mod_specs(shift_chunk, TM_EW)
    scb, scc = _mod_specs(shift_chunk + 1, TM_EW)
    return pl.pallas_call(
        _prenorm_kernel,
        grid=(T // TM_EW,),
        in_specs=[pl.BlockSpec((TM_EW, D), lambda i: (i, 0)),
                  pl.BlockSpec((1, D), lambda i: (0, 0)), shb, shc, scb, scc],
        out_specs=pl.BlockSpec((TM_EW, D), lambda i: (i, 0)),
        out_shape=jax.ShapeDtypeStruct((T, D), dtype),
        compiler_params=_cparams(("arbitrary",)),
    )(xs, norm_w.reshape(1, D), mods, mods, mods, mods)


SCAN_CHUNK = CTX
N_SCAN_CHUNKS = NZ // SCAN_CHUNK
SCAN_PARTS = 2
SUBS_PER_PART = SCAN_CHUNK // SCAN_SUB // SCAN_PARTS


def _bwd_chunk(ci):
    return jnp.where(ci == 0, 0, N_SCAN_CHUNKS - ci)


LB_PER_STEP = 2


def _scan_kernel(hf_ref, hb_ref, are_ref, aim_ref, drive_ref, read_ref, perm_ref, permt_ref,
                 yf_ref, yb_ref, sre_scr, sim_scr, bu_scr, wd_scr, wr_scr):
    @pl.when(pl.program_id(1) == 0)
    def _():
        sre_scr[...] = jnp.zeros_like(sre_scr)
        sim_scr[...] = jnp.zeros_like(sim_scr)
        gl = GROUPS_PER_LANE_BLOCK
        row_g = (lax.broadcasted_iota(jnp.int32, (2 * LANE, LANE), 0) % LANE) // SSM_CH
        lane_half = lax.broadcasted_iota(jnp.int32, (2 * LANE, LANE), 1) // SSM_STATE
        col_g = (lax.broadcasted_iota(jnp.int32, (SSM_STATE, 2 * LANE), 1) % LANE) // SSM_CH
        for jj in range(LB_PER_STEP):
            for r in range(2):
                for k in range(gl // 2):
                    tile = jnp.where(row_g == 2 * k + lane_half, drive_ref[jj, r], 0.0)
                    v = r * (gl // 2) + k
                    wd_scr[jj, :, v * LANE:(v + 1) * LANE] = tile.astype(BF16)
                for g in range(gl):
                    rows = slice((r * gl + g) * SSM_STATE, (r * gl + g + 1) * SSM_STATE)
                    wr_scr[jj, rows, :] = jnp.where(col_g == g, read_ref[jj, r], 0.0).astype(BF16)

    row = lax.broadcasted_iota(jnp.int32, (SCAN_ROWS, LANE), 0)
    is_fwd = (row % SCAN_SEQS) < B

    def windows(sub):
        off_f = sub * SCAN_SUB
        off_b = SCAN_CHUNK - SCAN_SUB - sub * SCAN_SUB
        return slice(off_f, off_f + SCAN_SUB), slice(off_b, off_b + SCAN_SUB)

    for jj in range(LB_PER_STEP):
        lanes = slice(jj * LANE, (jj + 1) * LANE)
        for part in range(SCAN_PARTS):
            lhs = []
            for q in range(SUBS_PER_PART):
                win_f, win_b = windows(part * SUBS_PER_PART + q)
                win = jnp.concatenate([hf_ref[:, win_f, lanes].reshape(B * SCAN_SUB, LANE),
                                       hb_ref[:, win_b, lanes].reshape(B * SCAN_SUB, LANE)], axis=0)
                u = jnp.dot(perm_ref[...], win, preferred_element_type=F32)
                zero = jnp.zeros_like(u)
                lhs.append(jnp.concatenate([jnp.where(is_fwd, u, zero), jnp.where(is_fwd, zero, u)],
                                           axis=1).astype(BF16))
            bu_scr[jj, part] = jnp.dot(jnp.concatenate(lhs, axis=0), wd_scr[jj], preferred_element_type=F32)

    for jj in range(LB_PER_STEP):
        lanes = slice(jj * LANE, (jj + 1) * LANE)
        a_re = are_ref[jj]
        a_im = aim_ref[jj]
        s_re = sre_scr[jj]
        s_im = sim_scr[jj]
        for part in range(SCAN_PARTS):
            bu = bu_scr.at[jj, part]
            for tau in range(SUBS_PER_PART * SCAN_SUB):
                rows = slice(tau * SCAN_SEQS, (tau + 1) * SCAN_SEQS)
                n_re = a_re * s_re - a_im * s_im + bu[rows, 0:STATE_LANES]
                n_im = a_re * s_im + a_im * s_re + bu[rows, STATE_LANES:2 * STATE_LANES]
                s_re, s_im = n_re, n_im
                bu[rows, 0:STATE_LANES] = s_re
                bu[rows, STATE_LANES:2 * STATE_LANES] = s_im
            half = SUBS_PER_PART * SCAN_ROWS // 2
            y2 = [jnp.dot(bu[r0:r0 + half, :].astype(BF16), wr_scr[jj], preferred_element_type=F32)
                  for r0 in (0, half)]
            for q in range(SUBS_PER_PART):
                win_f, win_b = windows(part * SUBS_PER_PART + q)
                r0 = q * SCAN_ROWS % half
                yq = y2[q * SCAN_ROWS // half][r0:r0 + SCAN_ROWS]
                y = jnp.where(is_fwd, yq[:, 0:LANE], yq[:, LANE:2 * LANE])
                yt = jnp.dot(permt_ref[...], y.astype(BF16),
                             preferred_element_type=F32).astype(BF16)
                yf_ref[:, win_f, lanes] = yt[0:B * SCAN_SUB].reshape(B, SCAN_SUB, LANE)
                yb_ref[:, win_b, lanes] = yt[B * SCAN_SUB:].reshape(B, SCAN_SUB, LANE)
        sre_scr[jj] = s_re
        sim_scr[jj] = s_im


def _ssm_scan(h, a_re_t, a_im_t, drive, read, perm, perm_t):
    h3 = h.reshape(B, NZ, D)
    lb = LB_PER_STEP
    blk = (B, SCAN_CHUNK, lb * LANE)
    fwd_spec = pl.BlockSpec(blk, lambda j, ci: (0, ci, j))
    bwd_spec = pl.BlockSpec(blk, lambda j, ci: (0, _bwd_chunk(ci), j))
    a_spec = pl.BlockSpec((lb, SCAN_SEQS, STATE_LANES), lambda j, ci: (j, 0, 0))
    p_spec = pl.BlockSpec((SCAN_ROWS, SCAN_ROWS), lambda j, ci: (0, 0))
    yf, yb = pl.pallas_call(
        _scan_kernel,
        grid=(N_LANE_BLOCKS // lb, N_SCAN_CHUNKS),
        in_specs=[fwd_spec, bwd_spec, a_spec, a_spec,
                  pl.BlockSpec((lb, 2, 2 * LANE, LANE), lambda j, ci: (j, 0, 0, 0)),
                  pl.BlockSpec((lb, 2, SSM_STATE, 2 * LANE), lambda j, ci: (j, 0, 0, 0)),
                  p_spec, p_spec],
        out_specs=[fwd_spec, bwd_spec],
        out_shape=[jax.ShapeDtypeStruct((B, NZ, D), BF16)] * 2,
        scratch_shapes=[pltpu.VMEM((lb, SCAN_SEQS, STATE_LANES), F32),
                        pltpu.VMEM((lb, SCAN_SEQS, STATE_LANES), F32),
                        pltpu.VMEM((lb, SCAN_PARTS, SUBS_PER_PART * SCAN_ROWS, 2 * STATE_LANES), F32),
                        pltpu.VMEM((lb, 2 * LANE, 2 * STATE_LANES), BF16),
                        pltpu.VMEM((lb, 2 * STATE_LANES, 2 * LANE), BF16)],
        compiler_params=_cparams(("arbitrary", "arbitrary")),
    )(h3, h3, a_re_t, a_im_t, drive, read, perm, perm_t)
    return yf.reshape(T, D), yb.reshape(T, D)


def _ssm_post_kernel(x_ref, nw_ref, shb_ref, shc_ref, scb_ref, scc_ref, d_ref, yf_ref, yb_ref, o_ref):
    for rows, is_ctx in _row_chunks(TM_EW):
        h = _norm_mod(x_ref[rows, :], nw_ref[...], _pick(is_ctx, shb_ref, shc_ref), _pick(is_ctx, scb_ref, scc_ref))
        y = d_ref[...] * h + yf_ref[rows, :] + yb_ref[rows, :]
        o_ref[rows, :] = jax.nn.gelu(y).astype(BF16)


def _ssm_post(xs, mods, norm_w, d_skip, yf, yb):
    shb, shc = _mod_specs(0, TM_EW)
    scb, scc = _mod_specs(1, TM_EW)
    row = pl.BlockSpec((TM_EW, D), lambda i: (i, 0))
    vec = pl.BlockSpec((1, D), lambda i: (0, 0))
    return pl.pallas_call(
        _ssm_post_kernel,
        grid=(T // TM_EW,),
        in_specs=[row, vec, shb, shc, scb, scc, vec, row, row],
        out_specs=row,
        out_shape=jax.ShapeDtypeStruct((T, D), BF16),
        compiler_params=_cparams(("arbitrary",)),
    )(xs, norm_w.reshape(1, D), mods, mods, mods, mods, d_skip.reshape(1, D), yf, yb)


def _glu_res_kernel(a_ref, wa_ref, wb_ref, ba_ref, bb_ref, x_ref, gb_ref, gc_ref, o_ref, w_scr):
    @pl.when(pl.program_id(1) == 0)
    def _():
        w_scr[0] = wa_ref[...].astype(BF16)
        w_scr[1] = wb_ref[...].astype(BF16)

    a = a_ref[...]
    is_ctx = _is_ctx_rows(TM_MM, axis=1)
    half = TN_MM // 2
    for c0 in (0, half):
        cols = slice(c0, c0 + half)
        za = jnp.dot(a, w_scr[0, :, cols], preferred_element_type=F32) + ba_ref[:, cols]
        zb = jnp.dot(a, w_scr[1, :, cols], preferred_element_type=F32) + bb_ref[:, cols]
        gate = jnp.where(is_ctx, gc_ref[0, :, cols], gb_ref[0, :, cols])
        o_ref[:, cols] = x_ref[:, cols] + gate * (za * jax.nn.sigmoid(zb))


def _glu_res(a, w_glu, layer, b_glu, xs, mods):
    gb, gc = _gate_specs_cols_outer(2)
    nb = D // TN_MM
    return pl.pallas_call(
        _glu_res_kernel,
        grid=(nb, T // TM_MM),
        in_specs=[pl.BlockSpec((TM_MM, D), lambda j, i: (i, 0)),
                  pl.BlockSpec((None, D, TN_MM), lambda j, i: (layer, 0, j)),
                  pl.BlockSpec((None, D, TN_MM), lambda j, i: (layer, 0, j + nb)),
                  pl.BlockSpec((1, TN_MM), lambda j, i: (0, j)),
                  pl.BlockSpec((1, TN_MM), lambda j, i: (0, j + nb)),
                  pl.BlockSpec((TM_MM, TN_MM), lambda j, i: (i, j)),
                  gb, gc],
        out_specs=pl.BlockSpec((TM_MM, TN_MM), lambda j, i: (i, j)),
        out_shape=jax.ShapeDtypeStruct((T, D), F32),
        scratch_shapes=[pltpu.VMEM((2, D, TN_MM), BF16)],
        compiler_params=_cparams(("arbitrary", "arbitrary")),
    )(a, w_glu, w_glu, b_glu.reshape(1, 2 * D), b_glu.reshape(1, 2 * D), xs, mods, mods)


def _tok_rows_load(ref, n):
    return jnp.concatenate([ref[pl.ds(j, n, stride=ROW_PITCH), :] for j in range(ROW_TILE)], axis=1)


def _tok_rows_store(ref, val, n):
    for j in range(ROW_TILE):
        ref[pl.ds(j, n, stride=ROW_TILE), :] = val[:, j * LANE:(j + 1) * LANE]


def _route_kernel(x_ref, nw_ref, shb_ref, shc_ref, scb_ref, scc_ref, wr_ref, br_ref,
                  h_ref, ri_ref, rw_ref, cnt_ref, carry_scr, *, lat_only):
    @pl.when(pl.program_id(0) == 0)
    def _():
        carry_scr[...] = jnp.zeros_like(carry_scr)

    is_ctx = _is_ctx_rows(TM_RT, lat_only)
    h = _norm_mod(x_ref[...], nw_ref[...], _pick(is_ctx, shb_ref, shc_ref), _pick(is_ctx, scb_ref, scc_ref))
    _tok_rows_store(h_ref, h, TM_RT)

    w = wr_ref[...]
    h_hi = h.astype(BF16)
    h_lo = (h - h_hi.astype(F32)).astype(BF16)
    w_hi = w.astype(BF16)
    w_lo = (w - w_hi.astype(F32)).astype(BF16)
    logits = (jnp.dot(h_hi, w_hi, preferred_element_type=F32)
              + jnp.dot(h_hi, w_lo, preferred_element_type=F32)
              + jnp.dot(h_lo, w_hi, preferred_element_type=F32)) + br_ref[...]

    lane = lax.broadcasted_iota(jnp.int32, (TM_RT, ROUTE_LANES), 1).astype(F32)
    big = float(ROUTE_LANES)
    neg = -jnp.inf
    is_g = lane < N_GROUPS
    g_max = jnp.max(jnp.where(is_g, logits, neg), axis=-1, keepdims=True)
    g_sum = jnp.sum(jnp.where(is_g, jnp.exp(logits - g_max), 0.0), axis=-1, keepdims=True)
    g_p = 1.0 / g_sum
    g_idx = jnp.min(jnp.where(is_g, jnp.where(logits == g_max, lane, big), big), axis=-1, keepdims=True)
    lo = N_GROUPS + N_EPG * g_idx
    e_log = jnp.where(lane >= lo, jnp.where(lane < lo + N_EPG, logits, neg), neg)
    e1 = jnp.max(e_log, axis=-1, keepdims=True)
    i1 = jnp.min(jnp.where(e_log == e1, lane, big), axis=-1, keepdims=True)
    e_log2 = jnp.where(lane == i1, neg, e_log)
    e2 = jnp.max(e_log2, axis=-1, keepdims=True)
    i2 = jnp.min(jnp.where(e_log2 == e2, lane, big), axis=-1, keepdims=True)
    p2 = jnp.exp(e2 - e1)
    w1 = g_p / (1.0 + p2)
    w2 = g_p * p2 / (1.0 + p2)
    x1 = i1 - N_GROUPS
    x2 = i2 - N_GROUPS

    sel1 = lane == x1
    sel2 = lane == x2
    onehot = jnp.where(sel1, 1.0, jnp.where(sel2, 1.0, 0.0))
    r_i = lax.broadcasted_iota(jnp.int32, (TM_RT, TM_RT), 0)
    c_i = lax.broadcasted_iota(jnp.int32, (TM_RT, TM_RT), 1)
    tril = jnp.where(r_i > c_i, 1.0, 0.0).astype(BF16)
    before = jnp.dot(tril, onehot.astype(BF16), preferred_element_type=F32) + carry_scr[0:1, :]
    rank1 = jnp.sum(jnp.where(sel1, before, 0.0), axis=-1, keepdims=True)
    rank2 = jnp.sum(jnp.where(sel2, before, 0.0), axis=-1, keepdims=True)
    total = carry_scr[0:1, :] + jnp.sum(onehot, axis=0, keepdims=True)
    carry_scr[...] = jnp.broadcast_to(total, carry_scr.shape)
    cnt_ref[...] = jnp.broadcast_to(total, cnt_ref.shape)

    ri = jnp.where(lane == 0, x1, jnp.where(lane == 1, x2, jnp.where(lane == 2, rank1, jnp.where(lane == 3, rank2, 0.0))))
    ri_ref[...] = ri.astype(jnp.int32)
    rw_ref[...] = jnp.where(lane == 0, w1, jnp.where(lane == 1, w2, 0.0))


def _route(xs, mods, norm_w, w_rg, b_rg, w_re, b_re, lat_only):
    n_tok = B * SEQ if lat_only else T
    pad = ROUTE_LANES - N_GROUPS - N_EXPERTS
    w_cat = jnp.concatenate([w_rg, w_re.reshape(D, N_EXPERTS), jnp.zeros((D, pad), F32)], axis=1)
    b_cat = jnp.concatenate([b_rg, b_re.reshape(N_EXPERTS), jnp.zeros((pad,), F32)]).reshape(1, ROUTE_LANES)
    shb, shc = _mod_specs(3, TM_RT, lat_only=lat_only)
    scb, scc = _mod_specs(4, TM_RT, lat_only=lat_only)
    lanes = pl.BlockSpec((TM_RT, ROUTE_LANES), lambda i: (i, 0))
    return pl.pallas_call(
        functools.partial(_route_kernel, lat_only=lat_only),
        grid=(n_tok // TM_RT,),
        in_specs=[pl.BlockSpec((TM_RT, D), lambda i: (_stream_tile(i, TM_RT, lat_only), 0)),
                  pl.BlockSpec((1, D), lambda i: (0, 0)), shb, shc, scb, scc,
                  pl.BlockSpec((D, ROUTE_LANES), lambda i: (0, 0)),
                  pl.BlockSpec((1, ROUTE_LANES), lambda i: (0, 0))],
        out_specs=[pl.BlockSpec((TM_RT * ROW_TILE, LANE), lambda i: (i, 0)), lanes, lanes,
                   pl.BlockSpec((SUBLANE, ROUTE_LANES), lambda i: (0, 0))],
        out_shape=[jax.ShapeDtypeStruct((n_tok * ROW_TILE, LANE), F32),
                   jax.ShapeDtypeStruct((n_tok, ROUTE_LANES), jnp.int32),
                   jax.ShapeDtypeStruct((n_tok, ROUTE_LANES), F32),
                   jax.ShapeDtypeStruct((SUBLANE, ROUTE_LANES), F32)],
        scratch_shapes=[pltpu.VMEM((SUBLANE, ROUTE_LANES), F32)],
        compiler_params=_cparams(("arbitrary",)),
    )(xs, norm_w.reshape(1, D), mods, mods, mods, mods, w_cat, b_cat)


def _row_gather(src_hbm, off_ref, off_index, dst, sem, n_groups, wait):
    def body(g, c):
        for i in range(GATHER_UNROLL):
            r = g * GATHER_UNROLL + i
            off = 0 if wait else pl.multiple_of(off_ref[off_index(r)], ROW_TILE)
            cp = pltpu.make_async_copy(src_hbm.at[pl.ds(off, ROW_TILE), :],
                                       dst.at[pl.ds(pl.multiple_of(r * ROW_PITCH, SUBLANE), ROW_TILE), :], sem)
            if wait:
                cp.wait()
            else:
                cp.start(priority=i % 2)
        return c

    lax.fori_loop(0, n_groups, body, 0)


def _expert_kernel(be_ref, nv_ref, bv_ref, ro_ref, h_hbm, w1_ref, w3_ref, w2_ref, y_ref, x_scr, sems):
    blk = pl.program_id(0)
    n_valid = nv_ref[0]
    slot = blk % 2

    def gather(block, into, wait):
        groups = (bv_ref[block] + (GATHER_UNROLL - 1)) // GATHER_UNROLL
        _row_gather(h_hbm, ro_ref, lambda r: block * TE + r, x_scr.at[into], sems.at[into], groups, wait)

    @pl.when(blk == 0)
    def _():
        x_scr[...] = jnp.zeros_like(x_scr)
        gather(0, 0, False)

    @pl.when(blk + 1 < n_valid)
    def _():
        gather(blk + 1, 1 - slot, False)

    @pl.when(blk < n_valid)
    def _():
        gather(blk, slot, True)
        x = _tok_rows_load(x_scr.at[slot], TE).astype(BF16)
        a = jnp.dot(x, w1_ref[0].astype(BF16), preferred_element_type=F32)
        c = jnp.dot(x, w3_ref[0].astype(BF16), preferred_element_type=F32)
        mid = (jax.nn.silu(a) * c).astype(BF16)
        _tok_rows_store(y_ref, jnp.dot(mid, w2_ref[0].astype(BF16), preferred_element_type=F32), TE)

    @pl.when(blk >= n_valid)
    def _():
        y_ref[...] = jnp.zeros_like(y_ref)


def _experts(h_rows, block_expert, n_valid, block_rows, row_off, w1, w3, w2):
    grid_spec = pltpu.PrefetchScalarGridSpec(
        num_scalar_prefetch=4,
        grid=(N_EBLOCKS,),
        in_specs=[pl.BlockSpec(memory_space=pl.ANY),
                  pl.BlockSpec((1, D, MOE_F), lambda b, be, nv, bv, ro: (be[b], 0, 0)),
                  pl.BlockSpec((1, D, MOE_F), lambda b, be, nv, bv, ro: (be[b], 0, 0)),
                  pl.BlockSpec((1, MOE_F, D), lambda b, be, nv, bv, ro: (be[b], 0, 0))],
        out_specs=pl.BlockSpec((TE * ROW_TILE, LANE), lambda b, be, nv, bv, ro: (jnp.minimum(b, nv[0]), 0)),
        scratch_shapes=[pltpu.VMEM((2, TE * ROW_PITCH, LANE), F32), pltpu.SemaphoreType.DMA((2,))],
    )
    return pl.pallas_call(
        _expert_kernel,
        grid_spec=grid_spec,
        out_shape=jax.ShapeDtypeStruct((N_EROWS * ROW_TILE, LANE), F32),
        compiler_params=_cparams(("arbitrary",)),
    )(block_expert, n_valid, block_rows, row_off, h_rows,
      w1.reshape(DEPTH * N_EXPERTS, D, MOE_F), w3.reshape(DEPTH * N_EXPERTS, D, MOE_F),
      w2.reshape(DEPTH * N_EXPERTS, MOE_F, D))


def _combine_kernel(do_ref, y_hbm, rw_ref, x_ref, gb_ref, gc_ref, o_ref, buf, sems, *, lat_only):
    tile = pl.program_id(0)
    slot = tile % 2

    def gather(t, into, wait):
        for k in range(MOE_TOPK):
            _row_gather(y_hbm, do_ref, lambda r: (t * TM_RT + r) * MOE_TOPK + k, buf.at[into, k],
                        sems.at[into], TM_RT // GATHER_UNROLL, wait)

    @pl.when(tile == 0)
    def _():
        gather(0, 0, False)

    @pl.when(tile + 1 < pl.num_programs(0))
    def _():
        gather(tile + 1, 1 - slot, False)

    gather(tile, slot, True)
    rw = rw_ref[...]
    y = (rw[:, 0:1] * _tok_rows_load(buf.at[slot, 0], TM_RT)
         + rw[:, 1:2] * _tok_rows_load(buf.at[slot, 1], TM_RT))
    gate = _pick(_is_ctx_rows(TM_RT, lat_only), gb_ref, gc_ref)
    o_ref[...] = x_ref[...] + gate * y


def _combine(ys_rows, dest_off, rw, xs, mods, lat_only):
    n_tok = B * SEQ if lat_only else T
    gb, gc = _mod_specs(5, TM_RT, lat_only=lat_only)
    grid_spec = pltpu.PrefetchScalarGridSpec(
        num_scalar_prefetch=1,
        grid=(n_tok // TM_RT,),
        in_specs=[pl.BlockSpec(memory_space=pl.ANY),
                  pl.BlockSpec((TM_RT, ROUTE_LANES), lambda i, *_: (i, 0)),
                  pl.BlockSpec((TM_RT, D), lambda i, *_: (_stream_tile(i, TM_RT, lat_only), 0)),
                  gb, gc],
        out_specs=pl.BlockSpec((TM_RT, D), lambda i, *_: (i, 0)),
        scratch_shapes=[pltpu.VMEM((2, MOE_TOPK, TM_RT * ROW_PITCH, LANE), F32),
                        pltpu.SemaphoreType.DMA((2,))],
    )
    return pl.pallas_call(
        functools.partial(_combine_kernel, lat_only=lat_only),
        grid_spec=grid_spec,
        out_shape=jax.ShapeDtypeStruct((n_tok, D), F32),
        compiler_params=_cparams(("arbitrary",)),
    )(dest_off, ys_rows, rw, xs, mods, mods)


def _moe(xs, mods, layer, norm_w, w_rg, b_rg, w_re, b_re, w1, w3, w2, lat_only=False):
    n_tok = B * SEQ if lat_only else T
    h_rows, ri, rw, cnt = _route(xs, mods, norm_w, w_rg, b_rg, w_re, b_re, lat_only)
    counts = cnt[0, :N_EXPERTS].astype(jnp.int32)
    padded = (counts + TE - 1) // TE * TE
    pad_end = jnp.cumsum(padded)
    pad_start = pad_end - padded
    dest = (pad_start[ri[:, 0:MOE_TOPK]] + ri[:, MOE_TOPK:2 * MOE_TOPK]).reshape(-1)
    row_off = jnp.zeros((N_EROWS,), jnp.int32).at[dest].set(
        jnp.repeat(jnp.arange(n_tok, dtype=jnp.int32) * ROW_TILE, MOE_TOPK))
    n_valid = pad_end[-1] // TE
    first_row = jnp.minimum(jnp.arange(N_EBLOCKS, dtype=jnp.int32), n_valid - 1) * TE
    block_expert = jnp.minimum(jnp.sum(pad_end[None, :] <= first_row[:, None], axis=1), N_EXPERTS - 1)
    block_rows = jnp.clip((pad_start + counts)[block_expert] - jnp.arange(N_EBLOCKS, dtype=jnp.int32) * TE, 0, TE)
    ys_rows = _experts(h_rows, (block_expert + layer * N_EXPERTS).astype(jnp.int32),
                       n_valid.reshape(1).astype(jnp.int32), block_rows.astype(jnp.int32), row_off, w1, w3, w2)
    return _combine(ys_rows, (dest * ROW_TILE).astype(jnp.int32), rw, xs, mods, lat_only)


def kernel(x, c, ctx, c_ctx, ada_w, ada_b, norm1_w, norm2_w, attn_w_qkv, attn_q_norm, attn_k_norm, attn_lam_q1, attn_lam_k1, attn_lam_q2, attn_lam_k2, attn_subln, attn_w_o, ssm_a_re, ssm_a_im, ssm_log_dt, ssm_b_re, ssm_b_im, ssm_c_re, ssm_c_im, ssm_d, ssm_w_glu, ssm_b_glu, moe_w_rg, moe_b_rg, moe_w_re, moe_b_re, moe_w1, moe_w3, moe_w2):
    xs = jnp.concatenate([ctx, x], axis=1).reshape(T, D)
    mods_all = _ada_table(c, c_ctx, ada_w, ada_b)
    cos, sin = _rope_tables()
    perm, perm_t = _scan_perm()
    for i in range(DEPTH):
        j = i // 2
        mods = mods_all[i].reshape(MOD_ROWS * ADA_CHUNKS, 1, D)
        if i % 2 == 0:
            lam_init = 0.8 - 0.6 * math.exp(-0.3 * i)
            qk, v = _qkv(xs, mods, norm1_w[i], attn_w_qkv, j, attn_q_norm[j], attn_k_norm[j], cos, sin)
            lam_vecs = jnp.stack([attn_lam_q1[j], attn_lam_k1[j], attn_lam_q2[j], attn_lam_k2[j]])
            o = _attention(qk, v, lam_vecs, attn_subln[j], lam_init)
            xs = _proj_res(o, attn_w_o, j, xs, mods, 2)
        else:
            ops = _ssm_operators(ssm_a_re[j], ssm_a_im[j], ssm_log_dt[j], ssm_b_re[j], ssm_b_im[j],
                                 ssm_c_re[j], ssm_c_im[j])
            h = _prenorm(xs, mods, norm1_w[i], 0, BF16)
            yf, yb = _ssm_scan(h, *ops, perm, perm_t)
            g = _ssm_post(xs, mods, norm1_w[i], ssm_d[j], yf, yb)
            xs = _glu_res(g, ssm_w_glu, j, ssm_b_glu[j], xs, mods)
        xs = _moe(xs, mods, i, norm2_w[i], moe_w_rg[i], moe_b_rg[i], moe_w_re[i], moe_b_re[i],
                  moe_w1, moe_w3, moe_w2, lat_only=(i == DEPTH - 1))
    return xs.reshape(B, SEQ, D)
```
